```python
import jax, jax.numpy as jnp
from jax import lax
import numpy as np

D_MODEL = 1024
BATCH = 8
SEQ = 4096
DEPTH = 2

N_HEADS = 16
HEAD_DIM = 64
ATTN_DIM = N_HEADS * HEAD_DIM
ROT_DIM = HEAD_DIM // 4
ROPE_THETA = 500000.0
D_FF = 2816
FFN_RESIDUAL_WEIGHT = 0.5
N_A_LAYERS = DEPTH // 2
N_B_LAYERS = DEPTH - N_A_LAYERS
DILATION_PAIRS = ((128, 1), (512, 4), (2048, 16))
MOBA_BLOCK = 256
MOBA_TOPK = 3
MOBA_Q_CHUNK = 32
RMS_EPS = 1e-6

kernel_name = 'hybrid_dilated_moba_yoco'


def rmsnorm(x, g):
    xf = x.astype(jnp.float32)
    y = xf * lax.rsqrt(jnp.mean(xf * xf, axis=-1, keepdims=True) + RMS_EPS)
    return (y * g.astype(jnp.float32)).astype(x.dtype)


def swiglu(h, w_gate, w_up, w_down):
    return (jax.nn.silu(h @ w_gate) * (h @ w_up)) @ w_down


def partial_rotary(t, pos):
    half = ROT_DIM // 2
    inv_freq = ROPE_THETA ** (-jnp.arange(half, dtype=jnp.float32) / half)
    ang = pos.astype(jnp.float32)[:, None] * inv_freq[None, :]
    cos, sin = jnp.cos(ang), jnp.sin(ang)
    tf = t.astype(jnp.float32)
    x1, x2 = tf[..., :half], tf[..., half:ROT_DIM]
    out = jnp.concatenate([x1 * cos - x2 * sin, x1 * sin + x2 * cos, tf[..., ROT_DIM:]], axis=-1)
    return out.astype(t.dtype)


def dilated_branch(q, k, v, window, dil):
    B, H, S, Dh = q.shape
    blk = window // dil
    span = blk * dil
    s_pad = -(-S // span) * span
    L = s_pad // dil
    nb = L // blk

    def to_sub(t):
        t = jnp.pad(t, ((0, 0), (0, 0), (0, s_pad - S), (0, 0)))
        t = t.reshape(B, H, L, dil, Dh).transpose(0, 1, 3, 2, 4)
        return t.reshape(B, H, dil, nb, blk, Dh)

    qs, ks, vs = to_sub(q), to_sub(k), to_sub(v)

    def with_prev(t):
        prev = jnp.pad(t, ((0, 0), (0, 0), (0, 0), (1, 0), (0, 0), (0, 0)))[:, :, :, :-1]
        return jnp.concatenate([prev, t], axis=4)

    kk, vv = with_prev(ks), with_prev(vs)
    s = jnp.einsum('bhrnqd,bhrnkd->bhrnqk', qs, kk).astype(jnp.float32) * (HEAD_DIM ** -0.5)
    qi = jnp.arange(blk)[:, None]
    kj = jnp.arange(2 * blk)[None, :]
    diff = blk + qi - kj
    band = (diff >= 0) & (diff <= blk)
    has_prev = (jnp.arange(nb)[:, None, None] > 0) | (kj[None] >= blk)
    mask = band[None] & has_prev
    s = jnp.where(mask, s, -jnp.inf)
    m = jnp.max(s, axis=-1, keepdims=True)
    p = jnp.exp(s - m)
    l = jnp.sum(p, axis=-1, keepdims=True)
    o = jnp.einsum('bhrnqk,bhrnkd->bhrnqd', (p / l).astype(v.dtype), vv)
    lse = m + jnp.log(l)

    def from_sub(t):
        X = t.shape[-1]
        t = t.reshape(B, H, dil, L, X).transpose(0, 1, 3, 2, 4).reshape(B, H, s_pad, X)
        return t[:, :, :S]

    return from_sub(o), from_sub(lse)[..., 0]


def mixer_a(h, w_qkv, w_o, pos):
    B, S, _ = h.shape
    qkv = (h @ w_qkv).reshape(B, S, 3, N_HEADS, HEAD_DIM).transpose(2, 0, 3, 1, 4)
    q, k, v = partial_rotary(qkv[0], pos), partial_rotary(qkv[1], pos), qkv[2]
    outs, lses = [], []
    for window, dil in DILATION_PAIRS:
        o_i, lse_i = dilated_branch(q, k, v, window, dil)
        outs.append(o_i)
        lses.append(lse_i)
    wts = jax.nn.softmax(jnp.stack(lses, axis=0), axis=0)
    o = jnp.sum(wts[..., None] * jnp.stack(outs, axis=0).astype(jnp.float32), axis=0).astype(h.dtype)
    return o.transpose(0, 2, 1, 3).reshape(B, S, ATTN_DIM) @ w_o


def shared_kv(h_stream, kv_norm, kv_w, pos):
    B, S, _ = h_stream.shape
    hn = rmsnorm(h_stream, kv_norm)
    kv = (hn @ kv_w).reshape(B, S, 2, N_HEADS, HEAD_DIM).transpose(2, 0, 3, 1, 4)
    k, v = partial_rotary(kv[0], pos), kv[1]
    s_pad = -(-S // MOBA_BLOCK) * MOBA_BLOCK
    pad = ((0, 0), (0, 0), (0, s_pad - S), (0, 0))
    k_pad, v_pad = jnp.pad(k, pad), jnp.pad(v, pad)
    nb = s_pad // MOBA_BLOCK
    k_mean = jnp.mean(k_pad.reshape(B, N_HEADS, nb, MOBA_BLOCK, HEAD_DIM).astype(jnp.float32), axis=3)
    return k_pad, v_pad, k_mean.astype(k.dtype)


def mixer_b(h, w_q, w_o, k_pad, v_pad, k_mean, pos):
    B, S, _ = h.shape
    q = (h @ w_q).reshape(B, S, N_HEADS, HEAD_DIM).transpose(0, 2, 1, 3)
    q = partial_rotary(q, pos)
    nb = k_mean.shape[2]
    ksel = min(MOBA_TOPK, nb)
    k_blocks = k_pad.reshape(B, N_HEADS, nb, MOBA_BLOCK, HEAD_DIM)
    v_blocks = v_pad.reshape(B, N_HEADS, nb, MOBA_BLOCK, HEAD_DIM)
    n_chunks = S // MOBA_Q_CHUNK
    q_chunks = q.reshape(B, N_HEADS, n_chunks, MOBA_Q_CHUNK, HEAD_DIM).transpose(2, 0, 1, 3, 4)
    gather = jax.vmap(jax.vmap(lambda blocks, idx: blocks[idx]))
    scale = HEAD_DIM ** -0.5

    def chunk_attend(args):
        c, qc = args
        t = c * MOBA_Q_CHUNK + jnp.arange(MOBA_Q_CHUNK)
        own = (c * MOBA_Q_CHUNK) // MOBA_BLOCK
        gate = jnp.einsum('bhqd,bhnd->bhqn', qc, k_mean).astype(jnp.float32)
        gate = jnp.where(jnp.arange(nb) < own, gate, -jnp.inf)
        _, sel = lax.top_k(gate, ksel)
        valid = sel < own
        k_s = gather(k_blocks, sel)
        v_s = gather(v_blocks, sel)
        s_sel = jnp.einsum('bhqd,bhqnkd->bhqnk', qc, k_s).astype(jnp.float32) * scale
        s_sel = jnp.where(valid[..., None], s_sel, -jnp.inf).reshape(B, N_HEADS, MOBA_Q_CHUNK, ksel * MOBA_BLOCK)
        k_o = lax.dynamic_slice_in_dim(k_pad, own * MOBA_BLOCK, MOBA_BLOCK, axis=2)
        v_o = lax.dynamic_slice_in_dim(v_pad, own * MOBA_BLOCK, MOBA_BLOCK, axis=2)
        s_own = jnp.einsum('bhqd,bhkd->bhqk', qc, k_o).astype(jnp.float32) * scale
        causal = (own * MOBA_BLOCK + jnp.arange(MOBA_BLOCK))[None, :] <= t[:, None]
        s_own = jnp.where(causal, s_own, -jnp.inf)
        p = jax.nn.softmax(jnp.concatenate([s_sel, s_own], axis=-1), axis=-1).astype(qc.dtype)
        p_sel = p[..., :ksel * MOBA_BLOCK].reshape(B, N_HEADS, MOBA_Q_CHUNK, ksel, MOBA_BLOCK)
        p_own = p[..., ksel * MOBA_BLOCK:]
        return (jnp.einsum('bhqnk,bhqnkd->bhqd', p_sel, v_s)
                + jnp.einsum('bhqk,bhkd->bhqd', p_own, v_o))

    out = lax.map(chunk_attend, (jnp.arange(n_chunks), q_chunks))
    out = out.transpose(1, 0, 3, 2, 4).reshape(B, S, ATTN_DIM)
    return out @ w_o


def setup_inputs(seed: int = 0) -> dict:
    key = jax.random.key(seed)
    ks = jax.random.split(key, 18)
    f32 = jnp.float32

    def w(k, shape, fan_in):
        return jax.random.normal(k, shape, f32) * (fan_in ** -0.5)

    def gain(k, shape):
        return 1.0 + 0.02 * jax.random.normal(k, shape, f32)

    return {
        'x': jax.random.normal(ks[0], (BATCH, SEQ, D_MODEL), f32),
        'ffn1_norm': gain(ks[1], (DEPTH, D_MODEL)),
        'ffn1_w_gate': w(ks[2], (DEPTH, D_MODEL, D_FF), D_MODEL),
        'ffn1_w_up': w(ks[3], (DEPTH, D_MODEL, D_FF), D_MODEL),
        'ffn1_w_down': w(ks[4], (DEPTH, D_FF, D_MODEL), D_FF),
        'mix_norm': gain(ks[5], (DEPTH, D_MODEL)),
        'ffn2_norm': gain(ks[6], (DEPTH, D_MODEL)),
        'ffn2_w_gate': w(ks[7], (DEPTH, D_MODEL, D_FF), D_MODEL),
        'ffn2_w_up': w(ks[8], (DEPTH, D_MODEL, D_FF), D_MODEL),
        'ffn2_w_down': w(ks[9], (DEPTH, D_FF, D_MODEL), D_FF),
        'a_w_qkv': w(ks[10], (N_A_LAYERS, D_MODEL, 3 * ATTN_DIM), D_MODEL),
        'a_w_o': w(ks[11], (N_A_LAYERS, ATTN_DIM, D_MODEL), ATTN_DIM),
        'kv_norm': gain(ks[12], (D_MODEL,)),
        'kv_w': w(ks[13], (D_MODEL, 2 * ATTN_DIM), D_MODEL),
        'b_w_q': w(ks[14], (N_B_LAYERS, D_MODEL, ATTN_DIM), D_MODEL),
        'b_w_o': w(ks[15], (N_B_LAYERS, ATTN_DIM, D_MODEL), ATTN_DIM),
        'final_norm': gain(ks[16], (D_MODEL,)),
    }


def reference(x, ffn1_norm, ffn1_w_gate, ffn1_w_up, ffn1_w_down, mix_norm,
              ffn2_norm, ffn2_w_gate, ffn2_w_up, ffn2_w_down, a_w_qkv, a_w_o,
              kv_norm, kv_w, b_w_q, b_w_o, final_norm):
    pos = jnp.arange(x.shape[1], dtype=jnp.int32)
    h = x
    shared = None
    for layer in range(DEPTH):
        h = h + FFN_RESIDUAL_WEIGHT * swiglu(rmsnorm(h, ffn1_norm[layer]),
                                             ffn1_w_gate[layer], ffn1_w_up[layer], ffn1_w_down[layer])
        hn = rmsnorm(h, mix_norm[layer])
        if layer < N_A_LAYERS:
            h = h + mixer_a(hn, a_w_qkv[layer], a_w_o[layer], pos)
        else:
            j = layer - N_A_LAYERS
            h = h + mixer_b(hn, b_w_q[j], b_w_o[j], shared[0], shared[1], shared[2], pos)
        h = h + FFN_RESIDUAL_WEIGHT * swiglu(rmsnorm(h, ffn2_norm[layer]),
                                             ffn2_w_gate[layer], ffn2_w_up[layer], ffn2_w_down[layer])
        if layer == N_A_LAYERS - 1:
            shared = shared_kv(h, kv_norm, kv_w, pos)
    return rmsnorm(h, final_norm)
```

```python
import functools

import jax
import jax.numpy as jnp
from jax import lax
from jax.experimental import pallas as pl
from jax.experimental.pallas import tpu as pltpu

D_MODEL = 1024
N_HEADS = 16
HEAD_DIM = 64
ATTN_DIM = N_HEADS * HEAD_DIM
ROT_DIM = HEAD_DIM // 4
ROPE_THETA = 500000.0
D_FF = 2816
FFN_RESIDUAL_WEIGHT = 0.5
DILATION_PAIRS = ((128, 1), (512, 4), (2048, 16))
MOBA_BLOCK = 256
MOBA_TOPK = 3
RMS_EPS = 1e-6

LANES = 128
N_HEAD_PAIRS = ATTN_DIM // LANES
BAND_BLOCK = 128
MASKED_SCORE = -1e30
VMEM_LIMIT = 56 * 1024 * 1024

F32 = jnp.float32
BF16 = jnp.bfloat16


def _rms(x, g):
    return x * lax.rsqrt(jnp.mean(x * x, axis=-1, keepdims=True) + RMS_EPS) * g


def _resident(shape):
    zeros = (0,) * len(shape)
    return pl.BlockSpec(shape, lambda *_: zeros, pipeline_mode=pl.Buffered(1))


def _ffn_body(x_ref, g_ref, wg_ref, wu_ref, wd_ref, *rest, final_norm):
    if final_norm:
        fg_ref, o_ref = rest
    else:
        (o_ref,) = rest
    x = x_ref[...]
    hn = _rms(x, g_ref[...]).astype(BF16)
    gate = jnp.dot(hn, wg_ref[...], preferred_element_type=F32)
    up = jnp.dot(hn, wu_ref[...], preferred_element_type=F32)
    act = (jax.nn.silu(gate) * up).astype(BF16)
    y = x + FFN_RESIDUAL_WEIGHT * jnp.dot(act, wd_ref[...], preferred_element_type=F32)
    if final_norm:
        y = _rms(y, fg_ref[...])
    o_ref[...] = y


def _ffn(h, g, wg, wu, wd, final_g=None, *, tm=512):
    m = h.shape[0]
    row = pl.BlockSpec((tm, D_MODEL), lambda i: (i, 0))
    in_specs = [row, _resident((1, D_MODEL)), _resident((D_MODEL, D_FF)),
                _resident((D_MODEL, D_FF)), _resident((D_FF, D_MODEL))]
    args = [h, g.reshape(1, D_MODEL), wg, wu, wd]
    if final_g is not None:
        in_specs.append(_resident((1, D_MODEL)))
        args.append(final_g.reshape(1, D_MODEL))
    return pl.pallas_call(
        functools.partial(_ffn_body, final_norm=final_g is not None),
        grid=(m // tm,),
        in_specs=in_specs,
        out_specs=row,
        out_shape=jax.ShapeDtypeStruct((m, D_MODEL), F32),
        compiler_params=pltpu.CompilerParams(
            dimension_semantics=("parallel",), vmem_limit_bytes=VMEM_LIMIT),
        name="ffn",
    )(*args)


def _rotary_tables(seq, scale):
    half = ROT_DIM // 2
    inv_freq = ROPE_THETA ** (-jnp.arange(half, dtype=F32) / half)
    ang = jnp.arange(seq, dtype=F32)[:, None] * inv_freq[None, :]
    cos, sin = jnp.cos(ang), jnp.sin(ang)
    zeros = jnp.zeros((seq, HEAD_DIM - ROT_DIM), F32)
    z8 = jnp.zeros((seq, half), F32)
    c = jnp.concatenate([cos, cos, jnp.ones_like(zeros)], axis=1)
    s_up = jnp.concatenate([z8, sin, zeros], axis=1)
    s_dn = jnp.concatenate([-sin, z8, zeros], axis=1)
    tab = jnp.stack([c, s_up, s_dn], axis=0) * scale
    return jnp.tile(tab, (1, 1, LANES // HEAD_DIM))


def _proj_body(x_ref, g_ref, w_ref, rot_a_ref, rot_b_ref, o_ref, *rest, groups_a, groups_b, key_means):
    x = x_ref[...]
    hn = _rms(x, g_ref[...]).astype(BF16)
    y = jnp.dot(hn, w_ref[...], preferred_element_type=F32)
    tm, n = y.shape
    for j in range(n // LANES):
        t = y[:, j * LANES:(j + 1) * LANES]
        rot_ref = rot_a_ref if j < groups_a else (rot_b_ref if j < groups_a + groups_b else None)
        if rot_ref is not None:
            t = (t * rot_ref[0]
                 + pltpu.roll(t, ROT_DIM // 2, 1) * rot_ref[1]
                 + pltpu.roll(t, LANES - ROT_DIM // 2, 1) * rot_ref[2])
            if key_means and j >= groups_a:
                km_ref = rest[0]
                jk = j - groups_a
                km_ref[:, 0, jk * LANES:(jk + 1) * LANES] = jnp.mean(
                    t.reshape(tm // MOBA_BLOCK, MOBA_BLOCK, LANES), axis=1)
        o_ref[:, j * LANES:(j + 1) * LANES] = t.astype(BF16)


def _proj(h, g, w, rot_a, rot_b, *, groups_a, groups_b, key_means=False, tm=512):
    m = h.shape[0]
    n = w.shape[1]
    seq = rot_a.shape[1]
    tiles_per_seq = seq // tm
    rot_spec = pl.BlockSpec((3, tm, LANES), lambda i: (0, i % tiles_per_seq, 0))
    out_shape = [jax.ShapeDtypeStruct((m, n), BF16)]
    out_specs = [pl.BlockSpec((tm, n), lambda i: (i, 0))]
    if key_means:
        nb = tm // MOBA_BLOCK
        out_shape.append(jax.ShapeDtypeStruct((m // MOBA_BLOCK, 1, groups_b * LANES), F32))
        out_specs.append(pl.BlockSpec((nb, 1, groups_b * LANES), lambda i: (i, 0, 0)))
    return pl.pallas_call(
        functools.partial(_proj_body, groups_a=groups_a, groups_b=groups_b, key_means=key_means),
        grid=(m // tm,),
        in_specs=[pl.BlockSpec((tm, D_MODEL), lambda i: (i, 0)), _resident((1, D_MODEL)),
                  _resident((D_MODEL, n)), rot_spec, rot_spec],
        out_specs=out_specs,
        out_shape=out_shape,
        compiler_params=pltpu.CompilerParams(
            dimension_semantics=("parallel",), vmem_limit_bytes=VMEM_LIMIT),
        name="proj",
    )(h, g.reshape(1, D_MODEL), w, rot_a, rot_b)


def _attn_a_body(q_ref, k_ref, v_ref, o_ref, lse_ref):
    hp = pl.program_id(2)
    n_blocks = q_ref.shape[0] // BAND_BLOCK

    @pl.when(hp == 0)
    def _():
        lse_ref[...] = jnp.zeros_like(lse_ref)

    lane = lax.broadcasted_iota(jnp.int32, (BAND_BLOCK, LANES), 1)
    first_head = lane < HEAD_DIM
    qi = lax.broadcasted_iota(jnp.int32, (BAND_BLOCK, 2 * BAND_BLOCK), 0)
    kj = lax.broadcasted_iota(jnp.int32, (BAND_BLOCK, 2 * BAND_BLOCK), 1)

    def block(n, carry):
        q0 = pl.multiple_of(n * BAND_BLOCK, BAND_BLOCK)
        k0 = pl.multiple_of(jnp.maximum(n - 1, 0) * BAND_BLOCK, BAND_BLOCK)
        q = q_ref[pl.ds(q0, BAND_BLOCK), :]
        k = k_ref[pl.ds(k0, 2 * BAND_BLOCK), :]
        v = v_ref[pl.ds(k0, 2 * BAND_BLOCK), :]
        dist = (q0 - k0) + qi - kj
        band = (dist >= 0) & (dist <= BAND_BLOCK)
        lse_tile = lse_ref[pl.ds(q0, BAND_BLOCK), :]
        outs = []
        for head in range(2):
            mine = first_head if head == 0 else jnp.logical_not(first_head)
            qm = jnp.where(mine, q, jnp.zeros_like(q))
            s = lax.dot_general(qm, k, (((1,), (1,)), ((), ())), preferred_element_type=F32)
            s = jnp.where(band, s, -jnp.inf)
            m = jnp.max(s, axis=-1, keepdims=True)
            p = jnp.exp(s - m)
            l = jnp.sum(p, axis=-1, keepdims=True)
            outs.append(jnp.dot((p * (1.0 / l)).astype(BF16), v, preferred_element_type=F32))
            lse_tile = jnp.where(lane == 2 * hp + head, m + jnp.log(l), lse_tile)
        o_ref[pl.ds(q0, BAND_BLOCK), :] = jnp.where(first_head, outs[0], outs[1]).astype(BF16)
        lse_ref[pl.ds(q0, BAND_BLOCK), :] = lse_tile
        return carry

    lax.fori_loop(0, n_blocks, block, 0)


def _attn_a_branch(qkv, batch, seq, dil):
    sub = seq // dil
    groups = 3 * N_HEAD_PAIRS
    view = qkv.reshape(batch, sub, dil * 3 * ATTN_DIM)

    def in_spec(which):
        return pl.BlockSpec((None, sub, LANES),
                            lambda b, r, hp: (b, 0, r * groups + which * N_HEAD_PAIRS + hp))

    o, lse = pl.pallas_call(
        _attn_a_body,
        grid=(batch, dil, N_HEAD_PAIRS),
        in_specs=[in_spec(0), in_spec(1), in_spec(2)],
        out_specs=[pl.BlockSpec((None, sub, LANES), lambda b, r, hp: (b, 0, r * N_HEAD_PAIRS + hp)),
                   pl.BlockSpec((None, sub, LANES), lambda b, r, hp: (b, 0, r))],
        out_shape=[jax.ShapeDtypeStruct((batch, sub, dil * ATTN_DIM), BF16),
                   jax.ShapeDtypeStruct((batch, sub, dil * LANES), F32)],
        compiler_params=pltpu.CompilerParams(
            dimension_semantics=("parallel", "parallel", "arbitrary"), vmem_limit_bytes=VMEM_LIMIT),
        name=f"attn_a_d{dil}",
    )(view, view, view)
    return o.reshape(batch * seq, ATTN_DIM), lse.reshape(batch * seq, LANES)


def _split_bf16(x):
    hi = x.astype(BF16)
    return hi, (x - hi.astype(F32)).astype(BF16)


def _wo_a_body(h_ref, o1_ref, o2_ref, o3_ref, l1_ref, l2_ref, l3_ref, e_ref, w_ref, out_ref):
    lses = [l1_ref[...], l2_ref[...], l3_ref[...]]
    top = jnp.maximum(jnp.maximum(lses[0], lses[1]), lses[2])
    es = [jnp.exp(l - top) for l in lses]
    inv = 1.0 / (es[0] + es[1] + es[2])
    merged = None
    for e, o_ref in zip(es, (o1_ref, o2_ref, o3_ref)):
        hi, lo = _split_bf16(e * inv)
        wide = (jnp.dot(hi, e_ref[...], preferred_element_type=F32)
                + jnp.dot(lo, e_ref[...], preferred_element_type=F32))
        term = wide * o_ref[...].astype(F32)
        merged = term if merged is None else merged + term
    out_ref[...] = h_ref[...] + jnp.dot(merged.astype(BF16), w_ref[...], preferred_element_type=F32)


def _wo_a(h, outs, lses, w, *, tm=512):
    m = h.shape[0]
    row = pl.BlockSpec((tm, D_MODEL), lambda i: (i, 0))
    lrow = pl.BlockSpec((tm, LANES), lambda i: (i, 0))
    head_of_col = jnp.arange(ATTN_DIM, dtype=jnp.int32) // HEAD_DIM
    expand = (jnp.arange(LANES, dtype=jnp.int32)[:, None] == head_of_col[None, :]).astype(BF16)
    return pl.pallas_call(
        _wo_a_body,
        grid=(m // tm,),
        in_specs=[row, row, row, row, lrow, lrow, lrow,
                  _resident((LANES, ATTN_DIM)), _resident((ATTN_DIM, D_MODEL))],
        out_specs=row,
        out_shape=jax.ShapeDtypeStruct((m, D_MODEL), F32),
        compiler_params=pltpu.CompilerParams(
            dimension_semantics=("parallel",), vmem_limit_bytes=VMEM_LIMIT),
        name="wo_a",
    )(h, *outs, *lses, expand, w)


def _moba_body(q_ref, k_ref, v_ref, km_ref, o_ref):
    own = pl.program_id(2)
    q = q_ref[...]
    rows = q.shape[0]
    n_blocks = km_ref.shape[0]
    lane = lax.broadcasted_iota(jnp.int32, (rows, LANES), 1)
    first_head = lane < HEAD_DIM
    causal = (lax.broadcasted_iota(jnp.int32, (rows, MOBA_BLOCK), 1)
              <= lax.broadcasted_iota(jnp.int32, (rows, MOBA_BLOCK), 0))
    key_means = jnp.concatenate(
        [km_ref[...].astype(BF16), jnp.zeros((LANES - n_blocks, LANES), BF16)], axis=0)
    contract_last = (((1,), (1,)), ((), ()))
    own0 = pl.multiple_of(own * MOBA_BLOCK, MOBA_BLOCK)
    outs = []
    for head in range(2):
        mine = first_head if head == 0 else jnp.logical_not(first_head)
        qm = jnp.where(mine, q, jnp.zeros_like(q))
        gate = lax.dot_general(qm, key_means, contract_last, preferred_element_type=F32)
        gate = jnp.where(lane < own, gate, -jnp.inf)
        chosen = jnp.zeros(gate.shape, jnp.bool_)
        for _ in range(MOBA_TOPK):
            best = jnp.max(gate, axis=-1, keepdims=True)
            at = jnp.min(jnp.where(gate == best, lane, LANES), axis=-1, keepdims=True)
            hit = lane == at
            chosen = chosen | (hit & (best > -jnp.inf))
            gate = jnp.where(hit, -jnp.inf, gate)
        bias = jnp.where(chosen, 0.0, MASKED_SCORE).astype(BF16)
        q_aug = jnp.concatenate([qm, bias], axis=1)

        s = lax.dot_general(qm, k_ref[pl.ds(own0, MOBA_BLOCK), :], contract_last,
                            preferred_element_type=F32)
        s = jnp.where(causal, s, -jnp.inf)
        m0 = jnp.max(s, axis=-1, keepdims=True)
        p = jnp.exp(s - m0)
        l0 = jnp.sum(p, axis=-1, keepdims=True)
        acc0 = jnp.dot(p.astype(BF16), v_ref[pl.ds(own0, MOBA_BLOCK), :], preferred_element_type=F32)

        def past_block(n, carry):
            m, l, acc = carry
            k0 = pl.multiple_of(n * MOBA_BLOCK, MOBA_BLOCK)
            kn = k_ref[pl.ds(k0, MOBA_BLOCK), :]
            vn = v_ref[pl.ds(k0, MOBA_BLOCK), :]
            tag = (lax.broadcasted_iota(jnp.int32, (MOBA_BLOCK, LANES), 1) == n).astype(BF16)
            k_aug = jnp.concatenate([kn, tag], axis=1)
            s = lax.dot_general(q_aug, k_aug, contract_last, preferred_element_type=F32)
            m_new = jnp.maximum(m, jnp.max(s, axis=-1, keepdims=True))
            alpha = jnp.exp(m - m_new)
            p = jnp.exp(s - m_new)
            l = alpha * l + jnp.sum(p, axis=-1, keepdims=True)
            acc = alpha * acc + jnp.dot(p.astype(BF16), vn, preferred_element_type=F32)
            return m_new, l, acc

        _, l, acc = lax.fori_loop(0, own, past_block, (m0, l0, acc0))
        outs.append(acc * (1.0 / l))
    o_ref[...] = jnp.where(first_head, outs[0], outs[1]).astype(BF16)


def _moba(q, kv, key_means, batch, seq):
    n_blocks = seq // MOBA_BLOCK
    q3 = q.reshape(batch, seq, ATTN_DIM)
    kv3 = kv.reshape(batch, seq, 2 * ATTN_DIM)
    km3 = key_means.reshape(batch, n_blocks, ATTN_DIM)
    o = pl.pallas_call(
        _moba_body,
        grid=(batch, N_HEAD_PAIRS, n_blocks),
        in_specs=[pl.BlockSpec((None, MOBA_BLOCK, LANES), lambda b, hp, i: (b, i, hp)),
                  pl.BlockSpec((None, seq, LANES), lambda b, hp, i: (b, 0, hp)),
                  pl.BlockSpec((None, seq, LANES), lambda b, hp, i: (b, 0, N_HEAD_PAIRS + hp)),
                  pl.BlockSpec((None, n_blocks, LANES), lambda b, hp, i: (b, 0, hp))],
        out_specs=pl.BlockSpec((None, MOBA_BLOCK, LANES), lambda b, hp, i: (b, i, hp)),
        out_shape=jax.ShapeDtypeStruct((batch, seq, ATTN_DIM), BF16),
        compiler_params=pltpu.CompilerParams(
            dimension_semantics=("parallel", "parallel", "arbitrary"), vmem_limit_bytes=VMEM_LIMIT),
        name="moba",
    )(q3, kv3, kv3, km3)
    return o.reshape(batch * seq, ATTN_DIM)


def _wo_b_body(h_ref, o_ref, w_ref, out_ref):
    out_ref[...] = h_ref[...] + jnp.dot(o_ref[...], w_ref[...], preferred_element_type=F32)


def _wo_b(h, o, w, *, tm=512):
    m = h.shape[0]
    row = pl.BlockSpec((tm, D_MODEL), lambda i: (i, 0))
    return pl.pallas_call(
        _wo_b_body,
        grid=(m // tm,),
        in_specs=[row, row, _resident((ATTN_DIM, D_MODEL))],
        out_specs=row,
        out_shape=jax.ShapeDtypeStruct((m, D_MODEL), F32),
        compiler_params=pltpu.CompilerParams(
            dimension_semantics=("parallel",), vmem_limit_bytes=VMEM_LIMIT),
        name="wo_b",
    )(h, o, w)


def kernel(x, ffn1_norm, ffn1_w_gate, ffn1_w_up, ffn1_w_down, mix_norm, ffn2_norm, ffn2_w_gate,
           ffn2_w_up, ffn2_w_down, a_w_qkv, a_w_o, kv_norm, kv_w, b_w_q, b_w_o, final_norm):
    batch, seq, _ = x.shape
    bf = lambda w: w.astype(BF16)
    rot_q = _rotary_tables(seq, HEAD_DIM ** -0.5)
    rot_k = _rotary_tables(seq, 1.0)
    h = x.reshape(batch * seq, D_MODEL)

    h = _ffn(h, ffn1_norm[0], bf(ffn1_w_gate[0]), bf(ffn1_w_up[0]), bf(ffn1_w_down[0]))
    qkv = _proj(h, mix_norm[0], bf(a_w_qkv[0]), rot_q, rot_k,
                groups_a=N_HEAD_PAIRS, groups_b=N_HEAD_PAIRS)[0]
    branches = [_attn_a_branch(qkv, batch, seq, dil) for _, dil in DILATION_PAIRS]
    h = _wo_a(h, [o for o, _ in branches], [l for _, l in branches], bf(a_w_o[0]))
    h = _ffn(h, ffn2_norm[0], bf(ffn2_w_gate[0]), bf(ffn2_w_up[0]), bf(ffn2_w_down[0]))

    kv, key_means = _proj(h, kv_norm, bf(kv_w), rot_k, rot_k,
                          groups_a=0, groups_b=N_HEAD_PAIRS, key_means=True)

    h = _ffn(h, ffn1_norm[1], bf(ffn1_w_gate[1]), bf(ffn1_w_up[1]), bf(ffn1_w_down[1]))
    q = _proj(h, mix_norm[1], bf(b_w_q[0]), rot_q, rot_q, groups_a=N_HEAD_PAIRS, groups_b=0)[0]
    h = _wo_b(h, _moba(q, kv, key_means, batch, seq), bf(b_w_o[0]))
    h = _ffn(h, ffn2_norm[1], bf(ffn2_w_gate[1]), bf(ffn2_w_up[1]), bf(ffn2_w_down[1]),
             final_g=final_norm)
    return h.reshape(batch, seq, D_MODEL)
```

```python
import functools

import jax
import jax.numpy as jnp
from jax import lax
from jax.experimental import pallas as pl
from jax.experimental.pallas import tpu as pltpu

D_MODEL = 1024
N_HEADS = 16
HEAD_DIM = 64
ATTN_DIM = N_HEADS * HEAD_DIM
ROT_DIM = HEAD_DIM // 4
ROPE_THETA = 500000.0
D_FF = 2816
FFN_RESIDUAL_WEIGHT = 0.5
DILATION_PAIRS = ((128, 1), (512, 4), (2048, 16))
MOBA_BLOCK = 256
MOBA_TOPK = 3
RMS_EPS = 1e-6

LANES = 128
N_HEAD_PAIRS = ATTN_DIM // LANES
BAND_BLOCK = 128
MASKED_SCORE = -1e30
VMEM_LIMIT = 56 * 1024 * 1024

F32 = jnp.float32
BF16 = jnp.bfloat16


def _rms(x, g):
    return x * lax.rsqrt(jnp.mean(x * x, axis=-1, keepdims=True) + RMS_EPS) * g


def _resident(shape):
    zeros = (0,) * len(shape)
    return pl.BlockSpec(shape, lambda *_: zeros, pipeline_mode=pl.Buffered(1))


def _ffn_body(x_ref, g_ref, wg_ref, wu_ref, wd_ref, *rest, final_norm):
    if final_norm:
        fg_ref, o_ref = rest
    else:
        (o_ref,) = rest
    x = x_ref[...]
    hn = _rms(x, g_ref[...]).astype(BF16)
    gate = jnp.dot(hn, wg_ref[...], preferred_element_type=F32)
    up = jnp.dot(hn, wu_ref[...], preferred_element_type=F32)
    act = (jax.nn.silu(gate) * up).astype(BF16)
    y = x + FFN_RESIDUAL_WEIGHT * jnp.dot(act, wd_ref[...], preferred_element_type=F32)
    if final_norm:
        y = _rms(y, fg_ref[...])
    o_ref[...] = y


def _ffn(h, g, wg, wu, wd, final_g=None, *, tm=512):
    m = h.shape[0]
    row = pl.BlockSpec((tm, D_MODEL), lambda i: (i, 0))
    in_specs = [row, _resident((1, D_MODEL)), _resident((D_MODEL, D_FF)),
                _resident((D_MODEL, D_FF)), _resident((D_FF, D_MODEL))]
    args = [h, g.reshape(1, D_MODEL), wg, wu, wd]
    if final_g is not None:
        in_specs.append(_resident((1, D_MODEL)))
        args.append(final_g.reshape(1, D_MODEL))
    return pl.pallas_call(
        functools.partial(_ffn_body, final_norm=final_g is not None),
        grid=(m // tm,),
        in_specs=in_specs,
        out_specs=row,
        out_shape=jax.ShapeDtypeStruct((m, D_MODEL), F32),
        compiler_params=pltpu.CompilerParams(
            dimension_semantics=("parallel",), vmem_limit_bytes=VMEM_LIMIT),
        name="ffn",
    )(*args)


def _rotary_tables(seq, scale):
    half = ROT_DIM // 2
    inv_freq = ROPE_THETA ** (-jnp.arange(half, dtype=F32) / half)
    ang = jnp.arange(seq, dtype=F32)[:, None] * inv_freq[None, :]
    cos, sin = jnp.cos(ang), jnp.sin(ang)
    zeros = jnp.zeros((seq, HEAD_DIM - ROT_DIM), F32)
    z8 = jnp.zeros((seq, half), F32)
    c = jnp.concatenate([cos, cos, jnp.ones_like(zeros)], axis=1)
    s_up = jnp.concatenate([z8, sin, zeros], axis=1)
    s_dn = jnp.concatenate([-sin, z8, zeros], axis=1)
    tab = jnp.stack([c, s_up, s_dn], axis=0) * scale
    return jnp.tile(tab, (1, 1, LANES // HEAD_DIM))


def _proj_body(*refs, groups_a, groups_b, shared_kv):
    if shared_kv:
        x_ref, g_ref, w_ref, rot_a_ref, rot_b_ref, wvt_ref, o_ref, km_ref, vt_ref = refs
    else:
        x_ref, g_ref, w_ref, rot_a_ref, rot_b_ref, o_ref = refs
    x = x_ref[...]
    hn = _rms(x, g_ref[...]).astype(BF16)
    y = jnp.dot(hn, w_ref[...], preferred_element_type=F32)
    tm, n = y.shape
    for j in range(n // LANES):
        t = y[:, j * LANES:(j + 1) * LANES]
        rot_ref = rot_a_ref if j < groups_a else (rot_b_ref if j < groups_a + groups_b else None)
        if rot_ref is not None:
            t = (t * rot_ref[0]
                 + pltpu.roll(t, ROT_DIM // 2, 1) * rot_ref[1]
                 + pltpu.roll(t, LANES - ROT_DIM // 2, 1) * rot_ref[2])
            if shared_kv:
                km_ref[:, 0, j * LANES:(j + 1) * LANES] = jnp.mean(
                    t.reshape(tm // MOBA_BLOCK, MOBA_BLOCK, LANES), axis=1)
        o_ref[:, j * LANES:(j + 1) * LANES] = t.astype(BF16)
    if shared_kv:
        vt = lax.dot_general(wvt_ref[...], hn, (((1,), (1,)), ((), ())), preferred_element_type=F32)
        for j in range(tm // MOBA_BLOCK):
            vt_ref[j] = vt[:, j * MOBA_BLOCK:(j + 1) * MOBA_BLOCK].astype(BF16)


def _proj(h, g, w, rot_a, rot_b, *, groups_a, groups_b, values_t_w=None, tm=512):
    m = h.shape[0]
    n = w.shape[1]
    seq = rot_a.shape[1]
    tiles_per_seq = seq // tm
    shared_kv = values_t_w is not None
    rot_spec = pl.BlockSpec((3, tm, LANES), lambda i: (0, i % tiles_per_seq, 0))
    in_specs = [pl.BlockSpec((tm, D_MODEL), lambda i: (i, 0)), _resident((1, D_MODEL)),
                _resident((D_MODEL, n)), rot_spec, rot_spec]
    args = [h, g.reshape(1, D_MODEL), w, rot_a, rot_b]
    out_shape = [jax.ShapeDtypeStruct((m, n), BF16)]
    out_specs = [pl.BlockSpec((tm, n), lambda i: (i, 0))]
    if shared_kv:
        nb = tm // MOBA_BLOCK
        in_specs.append(_resident((ATTN_DIM, D_MODEL)))
        args.append(values_t_w)
        out_shape += [jax.ShapeDtypeStruct((m // MOBA_BLOCK, 1, n), F32),
                      jax.ShapeDtypeStruct((m // MOBA_BLOCK, ATTN_DIM, MOBA_BLOCK), BF16)]
        out_specs += [pl.BlockSpec((nb, 1, n), lambda i: (i, 0, 0)),
                      pl.BlockSpec((nb, ATTN_DIM, MOBA_BLOCK), lambda i: (i, 0, 0))]
    return pl.pallas_call(
        functools.partial(_proj_body, groups_a=groups_a, groups_b=groups_b, shared_kv=shared_kv),
        grid=(m // tm,),
        in_specs=in_specs,
        out_specs=out_specs,
        out_shape=out_shape,
        compiler_params=pltpu.CompilerParams(
            dimension_semantics=("parallel",), vmem_limit_bytes=VMEM_LIMIT),
        name="proj",
    )(*args)


def _attn_a_body(q_ref, k_ref, v_ref, o_ref, lse_ref):
    hp = pl.program_id(2)
    n_blocks = q_ref.shape[0] // BAND_BLOCK

    @pl.when(hp == 0)
    def _():
        lse_ref[...] = jnp.zeros_like(lse_ref)

    lane = lax.broadcasted_iota(jnp.int32, (BAND_BLOCK, LANES), 1)
    first_head = lane < HEAD_DIM
    qi = lax.broadcasted_iota(jnp.int32, (BAND_BLOCK, 2 * BAND_BLOCK), 0)
    kj = lax.broadcasted_iota(jnp.int32, (BAND_BLOCK, 2 * BAND_BLOCK), 1)

    def block(n, carry):
        q0 = pl.multiple_of(n * BAND_BLOCK, BAND_BLOCK)
        k0 = pl.multiple_of(jnp.maximum(n - 1, 0) * BAND_BLOCK, BAND_BLOCK)
        q = q_ref[pl.ds(q0, BAND_BLOCK), :]
        k = k_ref[pl.ds(k0, 2 * BAND_BLOCK), :]
        v = v_ref[pl.ds(k0, 2 * BAND_BLOCK), :]
        dist = (q0 - k0) + qi - kj
        band = (dist >= 0) & (dist <= BAND_BLOCK)
        lse_tile = lse_ref[pl.ds(q0, BAND_BLOCK), :]
        outs = []
        for head in range(2):
            mine = first_head if head == 0 else jnp.logical_not(first_head)
            qm = jnp.where(mine, q, jnp.zeros_like(q))
            s = lax.dot_general(qm, k, (((1,), (1,)), ((), ())), preferred_element_type=F32)
            s = jnp.where(band, s, -jnp.inf)
            m = jnp.max(s, axis=-1, keepdims=True)
            p = jnp.exp(s - m)
            l = jnp.sum(p, axis=-1, keepdims=True)
            outs.append(jnp.dot((p * (1.0 / l)).astype(BF16), v, preferred_element_type=F32))
            lse_tile = jnp.where(lane == 2 * hp + head, m + jnp.log(l), lse_tile)
        o_ref[pl.ds(q0, BAND_BLOCK), :] = jnp.where(first_head, outs[0], outs[1]).astype(BF16)
        lse_ref[pl.ds(q0, BAND_BLOCK), :] = lse_tile
        return carry

    lax.fori_loop(0, n_blocks, block, 0)


def _attn_a_branch(qkv, batch, seq, dil):
    sub = seq // dil
    groups = 3 * N_HEAD_PAIRS
    view = qkv.reshape(batch, sub, dil * 3 * ATTN_DIM)

    def in_spec(which):
        return pl.BlockSpec((None, sub, LANES),
                            lambda b, r, hp: (b, 0, r * groups + which * N_HEAD_PAIRS + hp))

    o, lse = pl.pallas_call(
        _attn_a_body,
        grid=(batch, dil, N_HEAD_PAIRS),
        in_specs=[in_spec(0), in_spec(1), in_spec(2)],
        out_specs=[pl.BlockSpec((None, sub, LANES), lambda b, r, hp: (b, 0, r * N_HEAD_PAIRS + hp)),
                   pl.BlockSpec((None, sub, LANES), lambda b, r, hp: (b, 0, r))],
        out_shape=[jax.ShapeDtypeStruct((batch, sub, dil * ATTN_DIM), BF16),
                   jax.ShapeDtypeStruct((batch, sub, dil * LANES), F32)],
        compiler_params=pltpu.CompilerParams(
            dimension_semantics=("parallel", "parallel", "arbitrary"), vmem_limit_bytes=VMEM_LIMIT),
        name=f"attn_a_d{dil}",
    )(view, view, view)
    return o.reshape(batch * seq, ATTN_DIM), lse.reshape(batch * seq, LANES)


def _split_bf16(x):
    hi = x.astype(BF16)
    return hi, (x - hi.astype(F32)).astype(BF16)


def _wo_a_body(h_ref, o1_ref, o2_ref, o3_ref, l1_ref, l2_ref, l3_ref, e_ref, w_ref, out_ref):
    lses = [l1_ref[...], l2_ref[...], l3_ref[...]]
    top = jnp.maximum(jnp.maximum(lses[0], lses[1]), lses[2])
    es = [jnp.exp(l - top) for l in lses]
    inv = 1.0 / (es[0] + es[1] + es[2])
    merged = None
    for e, o_ref in zip(es, (o1_ref, o2_ref, o3_ref)):
        hi, lo = _split_bf16(e * inv)
        wide = (jnp.dot(hi, e_ref[...], preferred_element_type=F32)
                + jnp.dot(lo, e_ref[...], preferred_element_type=F32))
        term = wide * o_ref[...].astype(F32)
        merged = term if merged is None else merged + term
    out_ref[...] = h_ref[...] + jnp.dot(merged.astype(BF16), w_ref[...], preferred_element_type=F32)


def _wo_a(h, outs, lses, w, *, tm=512):
    m = h.shape[0]
    row = pl.BlockSpec((tm, D_MODEL), lambda i: (i, 0))
    lrow = pl.BlockSpec((tm, LANES), lambda i: (i, 0))
    head_of_col = jnp.arange(ATTN_DIM, dtype=jnp.int32) // HEAD_DIM
    expand = (jnp.arange(LANES, dtype=jnp.int32)[:, None] == head_of_col[None, :]).astype(BF16)
    return pl.pallas_call(
        _wo_a_body,
        grid=(m // tm,),
        in_specs=[row, row, row, row, lrow, lrow, lrow,
                  _resident((LANES, ATTN_DIM)), _resident((ATTN_DIM, D_MODEL))],
        out_specs=row,
        out_shape=jax.ShapeDtypeStruct((m, D_MODEL), F32),
        compiler_params=pltpu.CompilerParams(
            dimension_semantics=("parallel",), vmem_limit_bytes=VMEM_LIMIT),
        name="wo_a",
    )(h, *outs, *lses, expand, w)


MOBA_GROUP = 2
ONES_ROWS = 16


def _moba_body(q_ref, k_ref, vt_ref, tag_ref, km_ref, o_ref, sa_ref, sb_ref):
    own = pl.program_id(2)
    q = q_ref[...]
    rows = q.shape[0]
    n_blocks = km_ref.shape[0]
    cols = 2 * rows
    contract_last = (((1,), (1,)), ((), ()))
    first_head = lax.broadcasted_iota(jnp.int32, (rows, LANES), 1) < HEAD_DIM
    zero = jnp.zeros_like(q)
    q_both = jnp.concatenate([jnp.where(first_head, q, zero), jnp.where(first_head, zero, q)], axis=0)

    gate = lax.dot_general(km_ref[...].astype(BF16), q_both, contract_last, preferred_element_type=F32)
    blk = lax.broadcasted_iota(jnp.int32, (n_blocks, cols), 0)
    gate = jnp.where(blk < own, gate, -jnp.inf)
    chosen = jnp.zeros(gate.shape, jnp.bool_)
    for _ in range(MOBA_TOPK):
        best = jnp.max(gate, axis=0, keepdims=True)
        at = jnp.min(jnp.where(gate == best, blk, n_blocks), axis=0, keepdims=True)
        hit = blk == at
        chosen = chosen | (hit & (best > -jnp.inf))
        gate = jnp.where(hit, -jnp.inf, gate)
    bias_t = jnp.concatenate([jnp.where(chosen, 0.0, MASKED_SCORE),
                              jnp.full((LANES - n_blocks, cols), MASKED_SCORE, F32)], axis=0)
    q_aug = jnp.concatenate([q_both, bias_t.T.astype(BF16)], axis=1)

    ones = jnp.ones((ONES_ROWS, MOBA_BLOCK), BF16)

    own0 = pl.multiple_of(own * MOBA_BLOCK, MOBA_BLOCK)
    s = lax.dot_general(k_ref[pl.ds(own0, MOBA_BLOCK), :], q_both, contract_last,
                        preferred_element_type=F32)
    key_pos = lax.broadcasted_iota(jnp.int32, (MOBA_BLOCK, cols), 0)
    query_pos = lax.broadcasted_iota(jnp.int32, (MOBA_BLOCK, cols), 1) & (rows - 1)
    s = jnp.where(key_pos <= query_pos, s, MASKED_SCORE)
    m0 = jnp.max(s, axis=0, keepdims=True)
    p = jnp.exp(s - m0).astype(BF16)
    acc0 = jnp.dot(jnp.concatenate([vt_ref[own], ones], axis=0), p, preferred_element_type=F32)

    group_keys = MOBA_GROUP * MOBA_BLOCK

    def scores(n0, s_ref):
        k0 = pl.multiple_of(n0 * MOBA_BLOCK, group_keys)
        k_aug = jnp.concatenate([k_ref[pl.ds(k0, group_keys), :], tag_ref[pl.ds(k0, group_keys), :]], axis=1)
        s = lax.dot_general(k_aug, q_aug, contract_last, preferred_element_type=F32)
        s_ref[...] = s
        return jnp.max(s, axis=0, keepdims=True)

    def absorb(n0, s_ref, s_max, m, acc):
        m_new = jnp.maximum(m, s_max)
        p = jnp.exp(s_ref[...] - m_new).astype(BF16)
        vt = jnp.concatenate([vt_ref[n0 + u] for u in range(MOBA_GROUP)], axis=1)
        vt = jnp.concatenate([vt, jnp.ones((ONES_ROWS, group_keys), BF16)], axis=0)
        return m_new, jnp.exp(m - m_new) * acc + jnp.dot(vt, p, preferred_element_type=F32)

    def past_pair(j, carry):
        m_a, acc_a, m_b, acc_b, max_a = carry
        n0 = 2 * MOBA_GROUP * j
        max_b = scores(n0 + MOBA_GROUP, sb_ref)
        m_a, acc_a = absorb(n0, sa_ref, max_a, m_a, acc_a)
        max_a = scores(jnp.minimum(n0 + 2 * MOBA_GROUP, n_blocks - MOBA_GROUP), sa_ref)
        m_b, acc_b = absorb(n0 + MOBA_GROUP, sb_ref, max_b, m_b, acc_b)
        return m_a, acc_a, m_b, acc_b, max_a

    n_pairs = (own + 2 * MOBA_GROUP - 1) // (2 * MOBA_GROUP)
    init = (m0, acc0, jnp.full_like(m0, MASKED_SCORE), jnp.zeros_like(acc0), scores(0, sa_ref))
    m_a, acc_a, m_b, acc_b, _ = lax.fori_loop(0, n_pairs, past_pair, init)
    top = jnp.maximum(m_a, m_b)
    acc = jnp.exp(m_a - top) * acc_a + jnp.exp(m_b - top) * acc_b
    out_t = acc[:LANES] * (1.0 / acc[LANES:LANES + 1])
    out_t = jnp.concatenate([out_t[:HEAD_DIM, :rows], out_t[HEAD_DIM:, rows:]], axis=0)
    o_ref[...] = out_t.T.astype(BF16)


def _moba(q, k, vt, key_means, batch, seq):
    n_blocks = seq // MOBA_BLOCK
    q3 = q.reshape(batch, seq, ATTN_DIM)
    k3 = k.reshape(batch, seq, ATTN_DIM)
    km3 = key_means.reshape(batch, n_blocks, ATTN_DIM)
    tag = (jnp.arange(seq, dtype=jnp.int32)[:, None] // MOBA_BLOCK
           == jnp.arange(LANES, dtype=jnp.int32)[None, :]).astype(BF16)
    o = pl.pallas_call(
        _moba_body,
        grid=(batch, N_HEAD_PAIRS, n_blocks),
        in_specs=[pl.BlockSpec((None, MOBA_BLOCK, LANES), lambda b, hp, i: (b, i, hp)),
                  pl.BlockSpec((None, seq, LANES), lambda b, hp, i: (b, 0, hp)),
                  pl.BlockSpec((n_blocks, LANES, MOBA_BLOCK), lambda b, hp, i: (b, hp, 0)),
                  _resident((seq, LANES)),
                  pl.BlockSpec((None, n_blocks, LANES), lambda b, hp, i: (b, 0, hp))],
        out_specs=pl.BlockSpec((None, MOBA_BLOCK, LANES), lambda b, hp, i: (b, i, hp)),
        out_shape=jax.ShapeDtypeStruct((batch, seq, ATTN_DIM), BF16),
        scratch_shapes=[pltpu.VMEM((MOBA_GROUP * MOBA_BLOCK, 2 * MOBA_BLOCK), F32)] * 2,
        compiler_params=pltpu.CompilerParams(
            dimension_semantics=("parallel", "parallel", "arbitrary"), vmem_limit_bytes=VMEM_LIMIT),
        name="moba",
    )(q3, k3, vt, tag, km3)
    return o.reshape(batch * seq, ATTN_DIM)


def _wo_b_body(h_ref, o_ref, w_ref, out_ref):
    out_ref[...] = h_ref[...] + jnp.dot(o_ref[...], w_ref[...], preferred_element_type=F32)


def _wo_b(h, o, w, *, tm=512):
    m = h.shape[0]
    row = pl.BlockSpec((tm, D_MODEL), lambda i: (i, 0))
    return pl.pallas_call(
        _wo_b_body,
        grid=(m // tm,),
        in_specs=[row, row, _resident((ATTN_DIM, D_MODEL))],
        out_specs=row,
        out_shape=jax.ShapeDtypeStruct((m, D_MODEL), F32),
        compiler_params=pltpu.CompilerParams(
            dimension_semantics=("parallel",), vmem_limit_bytes=VMEM_LIMIT),
        name="wo_b",
    )(h, o, w)


def kernel(x, ffn1_norm, ffn1_w_gate, ffn1_w_up, ffn1_w_down, mix_norm, ffn2_norm, ffn2_w_gate,
           ffn2_w_up, ffn2_w_down, a_w_qkv, a_w_o, kv_norm, kv_w, b_w_q, b_w_o, final_norm):
    batch, seq, _ = x.shape
    bf = lambda w: w.astype(BF16)
    rot_q = _rotary_tables(seq, HEAD_DIM ** -0.5)
    rot_k = _rotary_tables(seq, 1.0)
    h = x.reshape(batch * seq, D_MODEL)

    h = _ffn(h, ffn1_norm[0], bf(ffn1_w_gate[0]), bf(ffn1_w_up[0]), bf(ffn1_w_down[0]))
    qkv = _proj(h, mix_norm[0], bf(a_w_qkv[0]), rot_q, rot_k,
                groups_a=N_HEAD_PAIRS, groups_b=N_HEAD_PAIRS)[0]
    branches = [_attn_a_branch(qkv, batch, seq, dil) for _, dil in DILATION_PAIRS]
    h = _wo_a(h, [o for o, _ in branches], [l for _, l in branches], bf(a_w_o[0]))
    h = _ffn(h, ffn2_norm[0], bf(ffn2_w_gate[0]), bf(ffn2_w_up[0]), bf(ffn2_w_down[0]))

    k, key_means, vt = _proj(h, kv_norm, bf(kv_w[:, :ATTN_DIM]), rot_k, rot_k, groups_a=0,
                             groups_b=N_HEAD_PAIRS, values_t_w=bf(kv_w[:, ATTN_DIM:].T))

    h = _ffn(h, ffn1_norm[1], bf(ffn1_w_gate[1]), bf(ffn1_w_up[1]), bf(ffn1_w_down[1]))
    q = _proj(h, mix_norm[1], bf(b_w_q[0]), rot_q, rot_q, groups_a=N_HEAD_PAIRS, groups_b=0)[0]
    h = _wo_b(h, _moba(q, k, vt, key_means, batch, seq), bf(b_w_o[0]))
    h = _ffn(h, ffn2_norm[1], bf(ffn2_w_gate[1]), bf(ffn2_w_up[1]), bf(ffn2_w_down[1]),
             final_g=final_norm)
    return h.reshape(batch, seq, D_MODEL)
```

```python
import functools

import jax
import jax.numpy as jnp
from jax import lax
from jax.experimental import pallas as pl
from jax.experimental.pallas import tpu as pltpu

D_MODEL = 1024
N_HEADS = 16
HEAD_DIM = 64
ATTN_DIM = N_HEADS * HEAD_DIM
ROT_DIM = HEAD_DIM // 4
ROPE_THETA = 500000.0
D_FF = 2816
FFN_RESIDUAL_WEIGHT = 0.5
DILATION_PAIRS = ((128, 1), (512, 4), (2048, 16))
MOBA_BLOCK = 256
MOBA_TOPK = 3
RMS_EPS = 1e-6

LANES = 128
N_HEAD_PAIRS = ATTN_DIM // LANES
BAND_BLOCK = 128
A_CLASSES = 16
TOKEN_TILE = 512
MASKED_SCORE = -1e30
VMEM_LIMIT = 56 * 1024 * 1024

F32 = jnp.float32
BF16 = jnp.bfloat16


def _rms(x, g):
    return x * lax.rsqrt(jnp.mean(x * x, axis=-1, keepdims=True) + RMS_EPS) * g


def _resident(shape):
    zeros = (0,) * len(shape)
    return pl.BlockSpec(shape, lambda *_: zeros, pipeline_mode=pl.Buffered(1))


def _ffn_body(x_ref, g_ref, wg_ref, wu_ref, wd_ref, *rest, final_norm):
    if final_norm:
        fg_ref, o_ref = rest
    else:
        (o_ref,) = rest
    x = x_ref[...]
    hn = _rms(x, g_ref[...]).astype(BF16)
    gate = jnp.dot(hn, wg_ref[...], preferred_element_type=F32)
    up = jnp.dot(hn, wu_ref[...], preferred_element_type=F32)
    act = (jax.nn.silu(gate) * up).astype(BF16)
    y = x + FFN_RESIDUAL_WEIGHT * jnp.dot(act, wd_ref[...], preferred_element_type=F32)
    if final_norm:
        y = _rms(y, fg_ref[...])
    o_ref[...] = y


def _ffn(h, g, wg, wu, wd, final_g=None, *, tm=512):
    m = h.shape[0]
    row = pl.BlockSpec((tm, D_MODEL), lambda i: (i, 0))
    in_specs = [row, _resident((1, D_MODEL)), _resident((D_MODEL, D_FF)),
                _resident((D_MODEL, D_FF)), _resident((D_FF, D_MODEL))]
    args = [h, g.reshape(1, D_MODEL), wg, wu, wd]
    if final_g is not None:
        in_specs.append(_resident((1, D_MODEL)))
        args.append(final_g.reshape(1, D_MODEL))
    return pl.pallas_call(
        functools.partial(_ffn_body, final_norm=final_g is not None),
        grid=(m // tm,),
        in_specs=in_specs,
        out_specs=row,
        out_shape=jax.ShapeDtypeStruct((m, D_MODEL), F32),
        compiler_params=pltpu.CompilerParams(
            dimension_semantics=("parallel",), vmem_limit_bytes=VMEM_LIMIT),
        name="ffn",
    )(*args)


def _rotary_tables(seq, scale):
    half = ROT_DIM // 2
    inv_freq = ROPE_THETA ** (-jnp.arange(half, dtype=F32) / half)
    ang = jnp.arange(seq, dtype=F32)[:, None] * inv_freq[None, :]
    cos, sin = jnp.cos(ang), jnp.sin(ang)
    zeros = jnp.zeros((seq, HEAD_DIM - ROT_DIM), F32)
    z8 = jnp.zeros((seq, half), F32)
    c = jnp.concatenate([cos, cos, jnp.ones_like(zeros)], axis=1)
    s_up = jnp.concatenate([z8, sin, zeros], axis=1)
    s_dn = jnp.concatenate([-sin, z8, zeros], axis=1)
    tab = jnp.stack([c, s_up, s_dn], axis=0) * scale
    return jnp.tile(tab, (1, 1, LANES // HEAD_DIM))


def _proj_body(*refs, groups_a, groups_b, shared_kv, class_major):
    if shared_kv:
        x_ref, g_ref, w_ref, rot_a_ref, rot_b_ref, wvt_ref, o_ref, km_ref, vt_ref = refs
    elif class_major:
        x_ref, g_ref, w_ref, rot_a_ref, rot_b_ref, perm_ref, o_ref = refs
    else:
        x_ref, g_ref, w_ref, rot_a_ref, rot_b_ref, o_ref = refs
    x = x_ref[...]
    hn = _rms(x, g_ref[...]).astype(BF16)
    if class_major:
        hn = jnp.dot(perm_ref[...], hn, preferred_element_type=F32).astype(BF16)
    y = jnp.dot(hn, w_ref[...], preferred_element_type=F32)
    tm, n = y.shape
    for j in range(n // LANES):
        t = y[:, j * LANES:(j + 1) * LANES]
        rot_ref = rot_a_ref if j < groups_a else (rot_b_ref if j < groups_a + groups_b else None)
        if rot_ref is not None:
            t = (t * rot_ref[0]
                 + pltpu.roll(t, ROT_DIM // 2, 1) * rot_ref[1]
                 + pltpu.roll(t, LANES - ROT_DIM // 2, 1) * rot_ref[2])
            if shared_kv:
                km_ref[:, 0, j * LANES:(j + 1) * LANES] = jnp.mean(
                    t.reshape(tm // MOBA_BLOCK, MOBA_BLOCK, LANES), axis=1)
        if class_major:
            o_ref[:, :, j * LANES:(j + 1) * LANES] = t.reshape(
                A_CLASSES, tm // A_CLASSES, LANES).astype(BF16)
        else:
            o_ref[:, j * LANES:(j + 1) * LANES] = t.astype(BF16)
    if shared_kv:
        vt = lax.dot_general(wvt_ref[...], hn, (((1,), (1,)), ((), ())), preferred_element_type=F32)
        for j in range(tm // MOBA_BLOCK):
            vt_ref[j] = vt[:, j * MOBA_BLOCK:(j + 1) * MOBA_BLOCK].astype(BF16)


def _proj(h, g, w, rot_a, rot_b, *, groups_a, groups_b, values_t_w=None, class_perm=None, tm=TOKEN_TILE):
    m = h.shape[0]
    n = w.shape[1]
    seq = rot_a.shape[1]
    tiles_per_seq = seq // tm
    shared_kv = values_t_w is not None
    class_major = class_perm is not None
    rot_spec = pl.BlockSpec((3, tm, LANES), lambda i: (0, i % tiles_per_seq, 0))
    in_specs = [pl.BlockSpec((tm, D_MODEL), lambda i: (i, 0)), _resident((1, D_MODEL)),
                _resident((D_MODEL, n)), rot_spec, rot_spec]
    args = [h, g.reshape(1, D_MODEL), w, rot_a, rot_b]
    out_shape = [jax.ShapeDtypeStruct((m, n), BF16)]
    out_specs = [pl.BlockSpec((tm, n), lambda i: (i, 0))]
    if shared_kv:
        nb = tm // MOBA_BLOCK
        in_specs.append(_resident((ATTN_DIM, D_MODEL)))
        args.append(values_t_w)
        out_shape += [jax.ShapeDtypeStruct((m // MOBA_BLOCK, 1, n), F32),
                      jax.ShapeDtypeStruct((m // MOBA_BLOCK, ATTN_DIM, MOBA_BLOCK), BF16)]
        out_specs += [pl.BlockSpec((nb, 1, n), lambda i: (i, 0, 0)),
                      pl.BlockSpec((nb, ATTN_DIM, MOBA_BLOCK), lambda i: (i, 0, 0))]
    if class_major:
        in_specs.append(_resident((tm, tm)))
        args.append(class_perm)
        out_shape = [jax.ShapeDtypeStruct((m // seq, A_CLASSES, seq // A_CLASSES, n), BF16)]
        out_specs = [pl.BlockSpec((None, A_CLASSES, tm // A_CLASSES, n),
                                  lambda i: (i // tiles_per_seq, 0, i % tiles_per_seq, 0))]
    return pl.pallas_call(
        functools.partial(_proj_body, groups_a=groups_a, groups_b=groups_b, shared_kv=shared_kv,
                          class_major=class_major),
        grid=(m // tm,),
        in_specs=in_specs,
        out_specs=out_specs,
        out_shape=out_shape,
        compiler_params=pltpu.CompilerParams(
            dimension_semantics=("parallel",), vmem_limit_bytes=VMEM_LIMIT),
        name="proj",
    )(*args)


def _first_head(rows):
    return lax.broadcasted_iota(jnp.int32, (rows, LANES), 1) < HEAD_DIM


def _band_tile(q, k, v, bias):
    nq, nk = bias.shape
    fq, fk = _first_head(nq), _first_head(nk)
    zero = jnp.zeros_like(q)
    q_both = jnp.concatenate([jnp.where(fq, q, zero), jnp.where(fq, zero, q)], axis=0)
    s = lax.dot_general(q_both, k, (((1,), (1,)), ((), ())), preferred_element_type=F32)
    one = jnp.ones_like(v)
    outs, tops = [], []
    for head in range(2):
        sh = s[head * nq:(head + 1) * nq] + bias
        top = jnp.max(sh, axis=-1, keepdims=True)
        p = jnp.exp(sh - top).astype(BF16)
        vh = jnp.where(fk, v, one) if head == 0 else jnp.where(fk, one, v)
        outs.append(jnp.dot(p, vh, preferred_element_type=F32))
        tops.append(top)
    acc = jnp.where(fq, outs[0], outs[1])
    den = pltpu.roll(jnp.where(fq, outs[1], outs[0]), HEAD_DIM, 1)
    return acc, jnp.where(fq, tops[0], tops[1]), den


def _merge_softmax(acc_a, top_a, den_a, acc_b, top_b, den_b):
    top = jnp.maximum(top_a, top_b)
    wa, wb = jnp.exp(top_a - top), jnp.exp(top_b - top)
    return wa * acc_a + wb * acc_b, top, wa * den_a + wb * den_b


def _attn_a_body(q_ref, k_ref, v_ref, b16_ref, b4_ref, b1_ref, o_ref, acc_ref, top_ref, den_ref):
    n_idx = q_ref.shape[1]

    def class_tiles(r0, carry):
        for u in range(4):
            r = 4 * r0 + u
            acc, top, den = _band_tile(q_ref[r], k_ref[r], v_ref[r], b16_ref[...])
            acc_ref[r] = acc
            top_ref[r] = top
            den_ref[r] = den
        return carry

    lax.fori_loop(0, A_CLASSES // 4, class_tiles, 0)

    def tiles_d4(it, carry):
        i0 = pl.multiple_of(it * 32, 32)
        j0 = pl.multiple_of(jnp.maximum(it - 1, 0) * 32, 32)
        bias = b4_ref[jnp.minimum(it, 1)]
        for r4 in range(4):
            classes = [r4 + 4 * c for c in range(4)]
            q = jnp.concatenate([q_ref[r, pl.ds(i0, 32), :] for r in classes], axis=0)
            k = jnp.concatenate([k_ref[r, pl.ds(j0, 64), :] for r in classes], axis=0)
            v = jnp.concatenate([v_ref[r, pl.ds(j0, 64), :] for r in classes], axis=0)
            acc, top, den = _band_tile(q, k, v, bias)
            for c, r in enumerate(classes):
                at = (r, pl.ds(i0, 32), slice(None))
                part = slice(c * 32, (c + 1) * 32)
                acc_ref[at], top_ref[at], den_ref[at] = _merge_softmax(
                    acc_ref[at], top_ref[at], den_ref[at], acc[part], top[part], den[part])
        return carry

    lax.fori_loop(0, n_idx // 32, tiles_d4, 0)

    def tile_d1(it, first):
        i0 = pl.multiple_of(it * 16, 16)
        j0 = pl.multiple_of(jnp.maximum(it - 1, 0) * 16, 16)
        q = jnp.concatenate([q_ref[r, pl.ds(i0, 16), :] for r in range(A_CLASSES)], axis=0)
        k = jnp.concatenate([k_ref[r, pl.ds(j0, 32), :] for r in range(A_CLASSES)], axis=0)
        v = jnp.concatenate([v_ref[r, pl.ds(j0, 32), :] for r in range(A_CLASSES)], axis=0)
        acc, top, den = _band_tile(q, k, v, b1_ref[first])
        for r in range(A_CLASSES):
            at = (r, pl.ds(i0, 16), slice(None))
            part = slice(r * 16, (r + 1) * 16)
            acc_m, _, den_m = _merge_softmax(
                acc_ref[at], top_ref[at], den_ref[at], acc[part], top[part], den[part])
            o_ref[at] = (acc_m / den_m).astype(BF16)

    half = n_idx // 32

    def tiles_d1(it, carry):
        tile_d1(it, jnp.minimum(it, 1))
        tile_d1(it + half, 1)
        return carry

    lax.fori_loop(0, half, tiles_d1, 0)


def _band_bias(n_q_classes, q_len, k_len, dilation_step, offset):
    cq = jnp.arange(n_q_classes * q_len, dtype=jnp.int32) // q_len
    iq = jnp.arange(n_q_classes * q_len, dtype=jnp.int32) % q_len
    ck = jnp.arange(n_q_classes * k_len, dtype=jnp.int32) // k_len
    ik = jnp.arange(n_q_classes * k_len, dtype=jnp.int32) % k_len
    dist = dilation_step * (offset + iq[:, None] - ik[None, :]) + (cq[:, None] - ck[None, :])
    return jnp.where((dist >= 0) & (dist <= BAND_BLOCK), 0.0, MASKED_SCORE).astype(F32)


def _attn_a(qkv, batch, seq):
    n_idx = seq // A_CLASSES
    b16 = _band_bias(1, n_idx, n_idx, 1, 0)
    b4 = jnp.stack([_band_bias(4, 32, 64, 4, off) for off in (0, 32)])
    b1 = jnp.stack([_band_bias(A_CLASSES, 16, 32, A_CLASSES, off) for off in (0, 16)])

    def spec(which):
        return pl.BlockSpec((None, A_CLASSES, n_idx, LANES),
                            lambda b, hp: (b, 0, 0, which * N_HEAD_PAIRS + hp))

    return pl.pallas_call(
        _attn_a_body,
        grid=(batch, N_HEAD_PAIRS),
        in_specs=[spec(0), spec(1), spec(2),
                  _resident(b16.shape), _resident(b4.shape), _resident(b1.shape)],
        out_specs=pl.BlockSpec((None, A_CLASSES, n_idx, LANES), lambda b, hp: (b, 0, 0, hp)),
        out_shape=jax.ShapeDtypeStruct((batch, A_CLASSES, n_idx, ATTN_DIM), BF16),
        scratch_shapes=[pltpu.VMEM((A_CLASSES, n_idx, LANES), F32)] * 3,
        compiler_params=pltpu.CompilerParams(
            dimension_semantics=("parallel", "parallel"), vmem_limit_bytes=VMEM_LIMIT),
        name="attn_a",
    )(qkv, qkv, qkv, b16, b4, b1)


def _class_perm(tm):
    per = tm // A_CLASSES
    dst = jnp.arange(tm, dtype=jnp.int32)
    src = A_CLASSES * (dst % per) + dst // per
    return (src[:, None] == jnp.arange(tm, dtype=jnp.int32)[None, :]).astype(BF16)


def _wo_a_body(h_ref, o_ref, unperm_ref, w_ref, out_ref):
    tm = h_ref.shape[0]
    o = o_ref[...].reshape(tm, ATTN_DIM)
    o = jnp.dot(unperm_ref[...], o, preferred_element_type=F32).astype(BF16)
    out_ref[...] = h_ref[...] + jnp.dot(o, w_ref[...], preferred_element_type=F32)


def _wo_a(h, o, unperm, w, *, tm=TOKEN_TILE):
    m = h.shape[0]
    tiles_per_seq = o.shape[2] * A_CLASSES // tm
    row = pl.BlockSpec((tm, D_MODEL), lambda i: (i, 0))
    return pl.pallas_call(
        _wo_a_body,
        grid=(m // tm,),
        in_specs=[row,
                  pl.BlockSpec((None, A_CLASSES, tm // A_CLASSES, ATTN_DIM),
                               lambda i: (i // tiles_per_seq, 0, i % tiles_per_seq, 0)),
                  _resident((tm, tm)), _resident((ATTN_DIM, D_MODEL))],
        out_specs=row,
        out_shape=jax.ShapeDtypeStruct((m, D_MODEL), F32),
        compiler_params=pltpu.CompilerParams(
            dimension_semantics=("parallel",), vmem_limit_bytes=VMEM_LIMIT),
        name="wo_a",
    )(h, o, unperm, w)


MOBA_GROUP = 2
ONES_ROWS = 16


def _moba_body(q_ref, k_ref, vt_ref, tag_ref, km_ref, o_ref, sa_ref, sb_ref):
    own = pl.program_id(2)
    q = q_ref[...]
    rows = q.shape[0]
    n_blocks = km_ref.shape[0]
    cols = 2 * rows
    contract_last = (((1,), (1,)), ((), ()))
    first_head = lax.broadcasted_iota(jnp.int32, (rows, LANES), 1) < HEAD_DIM
    zero = jnp.zeros_like(q)
    q_both = jnp.concatenate([jnp.where(first_head, q, zero), jnp.where(first_head, zero, q)], axis=0)

    gate = lax.dot_general(km_ref[...].astype(BF16), q_both, contract_last, preferred_element_type=F32)
    blk = lax.broadcasted_iota(jnp.int32, (n_blocks, cols), 0)
    gate = jnp.where(blk < own, gate, -jnp.inf)
    chosen = jnp.zeros(gate.shape, jnp.bool_)
    for _ in range(MOBA_TOPK):
        best = jnp.max(gate, axis=0, keepdims=True)
        at = jnp.min(jnp.where(gate == best, blk, n_blocks), axis=0, keepdims=True)
        hit = blk == at
        chosen = chosen | (hit & (best > -jnp.inf))
        gate = jnp.where(hit, -jnp.inf, gate)
    bias_t = jnp.concatenate([jnp.where(chosen, 0.0, MASKED_SCORE),
                              jnp.full((LANES - n_blocks, cols), MASKED_SCORE, F32)], axis=0)
    q_aug = jnp.concatenate([q_both, bias_t.T.astype(BF16)], axis=1)

    ones = jnp.ones((ONES_ROWS, MOBA_BLOCK), BF16)

    own0 = pl.multiple_of(own * MOBA_BLOCK, MOBA_BLOCK)
    s = lax.dot_general(k_ref[pl.ds(own0, MOBA_BLOCK), :], q_both, contract_last,
                        preferred_element_type=F32)
    key_pos = lax.broadcasted_iota(jnp.int32, (MOBA_BLOCK, cols), 0)
    query_pos = lax.broadcasted_iota(jnp.int32, (MOBA_BLOCK, cols), 1) & (rows - 1)
    s = jnp.where(key_pos <= query_pos, s, MASKED_SCORE)
    m0 = jnp.max(s, axis=0, keepdims=True)
    p = jnp.exp(s - m0).astype(BF16)
    acc0 = jnp.dot(jnp.concatenate([vt_ref[own], ones], axis=0), p, preferred_element_type=F32)

    group_keys = MOBA_GROUP * MOBA_BLOCK

    def scores(n0, s_ref):
        k0 = pl.multiple_of(n0 * MOBA_BLOCK, group_keys)
        k_aug = jnp.concatenate([k_ref[pl.ds(k0, group_keys), :], tag_ref[pl.ds(k0, group_keys), :]], axis=1)
        s = lax.dot_general(k_aug, q_aug, contract_last, preferred_element_type=F32)
        s_ref[...] = s
        return jnp.max(s, axis=0, keepdims=True)

    def absorb(n0, s_ref, s_max, m, acc):
        m_new = jnp.maximum(m, s_max)
        p = jnp.exp(s_ref[...] - m_new).astype(BF16)
        vt = jnp.concatenate([vt_ref[n0 + u] for u in range(MOBA_GROUP)], axis=1)
        vt = jnp.concatenate([vt, jnp.ones((ONES_ROWS, group_keys), BF16)], axis=0)
        return m_new, jnp.exp(m - m_new) * acc + jnp.dot(vt, p, preferred_element_type=F32)

    def past_pair(j, carry):
        m_a, acc_a, m_b, acc_b, max_a = carry
        n0 = 2 * MOBA_GROUP * j
        max_b = scores(n0 + MOBA_GROUP, sb_ref)
        m_a, acc_a = absorb(n0, sa_ref, max_a, m_a, acc_a)
        max_a = scores(jnp.minimum(n0 + 2 * MOBA_GROUP, n_blocks - MOBA_GROUP), sa_ref)
        m_b, acc_b = absorb(n0 + MOBA_GROUP, sb_ref, max_b, m_b, acc_b)
        return m_a, acc_a, m_b, acc_b, max_a

    n_pairs = (own + 2 * MOBA_GROUP - 1) // (2 * MOBA_GROUP)
    init = (m0, acc0, jnp.full_like(m0, MASKED_SCORE), jnp.zeros_like(acc0), scores(0, sa_ref))
    m_a, acc_a, m_b, acc_b, _ = lax.fori_loop(0, n_pairs, past_pair, init)
    top = jnp.maximum(m_a, m_b)
    acc = jnp.exp(m_a - top) * acc_a + jnp.exp(m_b - top) * acc_b
    out_t = acc[:LANES] * (1.0 / acc[LANES:LANES + 1])
    out_t = jnp.concatenate([out_t[:HEAD_DIM, :rows], out_t[HEAD_DIM:, rows:]], axis=0)
    o_ref[...] = out_t.T.astype(BF16)


def _moba(q, k, vt, key_means, batch, seq):
    n_blocks = seq // MOBA_BLOCK
    q3 = q.reshape(batch, seq, ATTN_DIM)
    k3 = k.reshape(batch, seq, ATTN_DIM)
    km3 = key_means.reshape(batch, n_blocks, ATTN_DIM)
    tag = (jnp.arange(seq, dtype=jnp.int32)[:, None] // MOBA_BLOCK
           == jnp.arange(LANES, dtype=jnp.int32)[None, :]).astype(BF16)
    o = pl.pallas_call(
        _moba_body,
        grid=(batch, N_HEAD_PAIRS, n_blocks),
        in_specs=[pl.BlockSpec((None, MOBA_BLOCK, LANES), lambda b, hp, i: (b, i, hp)),
                  pl.BlockSpec((None, seq, LANES), lambda b, hp, i: (b, 0, hp)),
                  pl.BlockSpec((n_blocks, LANES, MOBA_BLOCK), lambda b, hp, i: (b, hp, 0)),
                  _resident((seq, LANES)),
                  pl.BlockSpec((None, n_blocks, LANES), lambda b, hp, i: (b, 0, hp))],
        out_specs=pl.BlockSpec((None, MOBA_BLOCK, LANES), lambda b, hp, i: (b, i, hp)),
        out_shape=jax.ShapeDtypeStruct((batch, seq, ATTN_DIM), BF16),
        scratch_shapes=[pltpu.VMEM((MOBA_GROUP * MOBA_BLOCK, 2 * MOBA_BLOCK), F32)] * 2,
        compiler_params=pltpu.CompilerParams(
            dimension_semantics=("parallel", "parallel", "arbitrary"), vmem_limit_bytes=VMEM_LIMIT),
        name="moba",
    )(q3, k3, vt, tag, km3)
    return o.reshape(batch * seq, ATTN_DIM)


def _wo_b_body(h_ref, o_ref, w_ref, out_ref):
    out_ref[...] = h_ref[...] + jnp.dot(o_ref[...], w_ref[...], preferred_element_type=F32)


def _wo_b(h, o, w, *, tm=512):
    m = h.shape[0]
    row = pl.BlockSpec((tm, D_MODEL), lambda i: (i, 0))
    return pl.pallas_call(
        _wo_b_body,
        grid=(m // tm,),
        in_specs=[row, row, _resident((ATTN_DIM, D_MODEL))],
        out_specs=row,
        out_shape=jax.ShapeDtypeStruct((m, D_MODEL), F32),
        compiler_params=pltpu.CompilerParams(
            dimension_semantics=("parallel",), vmem_limit_bytes=VMEM_LIMIT),
        name="wo_b",
    )(h, o, w)


def kernel(x, ffn1_norm, ffn1_w_gate, ffn1_w_up, ffn1_w_down, mix_norm, ffn2_norm, ffn2_w_gate,
           ffn2_w_up, ffn2_w_down, a_w_qkv, a_w_o, kv_norm, kv_w, b_w_q, b_w_o, final_norm):
    batch, seq, _ = x.shape
    bf = lambda w: w.astype(BF16)
    rot_q = _rotary_tables(seq, HEAD_DIM ** -0.5)
    rot_k = _rotary_tables(seq, 1.0)
    h = x.reshape(batch * seq, D_MODEL)

    h = _ffn(h, ffn1_norm[0], bf(ffn1_w_gate[0]), bf(ffn1_w_up[0]), bf(ffn1_w_down[0]))
    perm = _class_perm(TOKEN_TILE)
    per = TOKEN_TILE // A_CLASSES
    order = jnp.arange(seq, dtype=jnp.int32).reshape(seq // TOKEN_TILE, per, A_CLASSES)
    order = order.transpose(0, 2, 1).reshape(seq)
    qkv = _proj(h, mix_norm[0], bf(a_w_qkv[0]), rot_q[:, order], rot_k[:, order],
                groups_a=N_HEAD_PAIRS, groups_b=N_HEAD_PAIRS, class_perm=perm)[0]
    h = _wo_a(h, _attn_a(qkv, batch, seq), perm.T, bf(a_w_o[0]))
    h = _ffn(h, ffn2_norm[0], bf(ffn2_w_gate[0]), bf(ffn2_w_up[0]), bf(ffn2_w_down[0]))

    k, key_means, vt = _proj(h, kv_norm, bf(kv_w[:, :ATTN_DIM]), rot_k, rot_k, groups_a=0,
                             groups_b=N_HEAD_PAIRS, values_t_w=bf(kv_w[:, ATTN_DIM:].T))

    h = _ffn(h, ffn1_norm[1], bf(ffn1_w_gate[1]), bf(ffn1_w_up[1]), bf(ffn1_w_down[1]))
    q = _proj(h, mix_norm[1], bf(b_w_q[0]), rot_q, rot_q, groups_a=N_HEAD_PAIRS, groups_b=0)[0]
    h = _wo_b(h, _moba(q, k, vt, key_means, batch, seq), bf(b_w_o[0]))
    h = _ffn(h, ffn2_norm[1], bf(ffn2_w_gate[1]), bf(ffn2_w_up[1]), bf(ffn2_w_down[1]),
             final_g=final_norm)
    return h.reshape(batch, seq, D_MODEL)
```

```python
import functools

import jax
import jax.numpy as jnp
from jax import lax
from jax.experimental import pallas as pl
from jax.experimental.pallas import tpu as pltpu

D_MODEL = 1024
N_HEADS = 16
HEAD_DIM = 64
ATTN_DIM = N_HEADS * HEAD_DIM
ROT_DIM = HEAD_DIM // 4
ROPE_THETA = 500000.0
D_FF = 2816
FFN_RESIDUAL_WEIGHT = 0.5
DILATION_PAIRS = ((128, 1), (512, 4), (2048, 16))
MOBA_BLOCK = 256
MOBA_TOPK = 3
RMS_EPS = 1e-6

LANES = 128
N_HEAD_PAIRS = ATTN_DIM // LANES
BAND_BLOCK = 128
A_CLASSES = 16
TOKEN_TILE = 512
MASKED_SCORE = -1e30
VMEM_LIMIT = 56 * 1024 * 1024

F32 = jnp.float32
BF16 = jnp.bfloat16


def _rms(x, g):
    return x * lax.rsqrt(jnp.mean(x * x, axis=-1, keepdims=True) + RMS_EPS) * g


def _resident(shape):
    zeros = (0,) * len(shape)
    return pl.BlockSpec(shape, lambda *_: zeros, pipeline_mode=pl.Buffered(1))


def _ffn_body(x_ref, g_ref, wg_ref, wu_ref, wd_ref, *rest, final_norm):
    if final_norm:
        fg_ref, o_ref = rest
    else:
        (o_ref,) = rest
    x = x_ref[...]
    hn = _rms(x, g_ref[...]).astype(BF16)
    gate = jnp.dot(hn, wg_ref[...], preferred_element_type=F32)
    up = jnp.dot(hn, wu_ref[...], preferred_element_type=F32)
    act = (jax.nn.silu(gate) * up).astype(BF16)
    y = x + FFN_RESIDUAL_WEIGHT * jnp.dot(act, wd_ref[...], preferred_element_type=F32)
    if final_norm:
        y = _rms(y, fg_ref[...])
    o_ref[...] = y


def _ffn(h, g, wg, wu, wd, final_g=None, *, tm=512):
    m = h.shape[0]
    row = pl.BlockSpec((tm, D_MODEL), lambda i: (i, 0))
    in_specs = [row, _resident((1, D_MODEL)), _resident((D_MODEL, D_FF)),
                _resident((D_MODEL, D_FF)), _resident((D_FF, D_MODEL))]
    args = [h, g.reshape(1, D_MODEL), wg, wu, wd]
    if final_g is not None:
        in_specs.append(_resident((1, D_MODEL)))
        args.append(final_g.reshape(1, D_MODEL))
    return pl.pallas_call(
        functools.partial(_ffn_body, final_norm=final_g is not None),
        grid=(m // tm,),
        in_specs=in_specs,
        out_specs=row,
        out_shape=jax.ShapeDtypeStruct((m, D_MODEL), F32),
        compiler_params=pltpu.CompilerParams(
            dimension_semantics=("parallel",), vmem_limit_bytes=VMEM_LIMIT),
        name="ffn",
    )(*args)


def _rotary_tables(seq, scale):
    half = ROT_DIM // 2
    inv_freq = ROPE_THETA ** (-jnp.arange(half, dtype=F32) / half)
    ang = jnp.arange(seq, dtype=F32)[:, None] * inv_freq[None, :]
    cos, sin = jnp.cos(ang), jnp.sin(ang)
    zeros = jnp.zeros((seq, HEAD_DIM - ROT_DIM), F32)
    z8 = jnp.zeros((seq, half), F32)
    c = jnp.concatenate([cos, cos, jnp.ones_like(zeros)], axis=1)
    s_up = jnp.concatenate([z8, sin, zeros], axis=1)
    s_dn = jnp.concatenate([-sin, z8, zeros], axis=1)
    tab = jnp.stack([c, s_up, s_dn], axis=0) * scale
    return jnp.tile(tab, (1, 1, LANES // HEAD_DIM))


def _proj_body(*refs, groups_a, groups_b, shared_kv, class_major):
    if shared_kv:
        x_ref, g_ref, w_ref, rot_a_ref, rot_b_ref, wvt_ref, o_ref, km_ref, vt_ref = refs
    elif class_major:
        x_ref, g_ref, w_ref, rot_a_ref, rot_b_ref, perm_ref, o_ref = refs
    else:
        x_ref, g_ref, w_ref, rot_a_ref, rot_b_ref, o_ref = refs
    x = x_ref[...]
    hn = _rms(x, g_ref[...]).astype(BF16)
    if class_major:
        hn = jnp.dot(perm_ref[...], hn, preferred_element_type=F32).astype(BF16)
    y = jnp.dot(hn, w_ref[...], preferred_element_type=F32)
    tm, n = y.shape
    for j in range(n // LANES):
        t = y[:, j * LANES:(j + 1) * LANES]
        rot_ref = rot_a_ref if j < groups_a else (rot_b_ref if j < groups_a + groups_b else None)
        if rot_ref is not None:
            t = (t * rot_ref[0]
                 + pltpu.roll(t, ROT_DIM // 2, 1) * rot_ref[1]
                 + pltpu.roll(t, LANES - ROT_DIM // 2, 1) * rot_ref[2])
            if shared_kv:
                km_ref[:, 0, j * LANES:(j + 1) * LANES] = jnp.mean(
                    t.reshape(tm // MOBA_BLOCK, MOBA_BLOCK, LANES), axis=1)
        if class_major:
            o_ref[:, :, j * LANES:(j + 1) * LANES] = t.reshape(
                A_CLASSES, tm // A_CLASSES, LANES).astype(BF16)
        else:
            o_ref[:, j * LANES:(j + 1) * LANES] = t.astype(BF16)
    if shared_kv:
        vt = lax.dot_general(wvt_ref[...], hn, (((1,), (1,)), ((), ())), preferred_element_type=F32)
        for j in range(tm // MOBA_BLOCK):
            vt_ref[j] = vt[:, j * MOBA_BLOCK:(j + 1) * MOBA_BLOCK].astype(BF16)


def _proj(h, g, w, rot_a, rot_b, *, groups_a, groups_b, values_t_w=None, class_perm=None, tm=TOKEN_TILE):
    m = h.shape[0]
    n = w.shape[1]
    seq = rot_a.shape[1]
    tiles_per_seq = seq // tm
    shared_kv = values_t_w is not None
    class_major = class_perm is not None
    rot_spec = pl.BlockSpec((3, tm, LANES), lambda i: (0, i % tiles_per_seq, 0))
    in_specs = [pl.BlockSpec((tm, D_MODEL), lambda i: (i, 0)), _resident((1, D_MODEL)),
                _resident((D_MODEL, n)), rot_spec, rot_spec]
    args = [h, g.reshape(1, D_MODEL), w, rot_a, rot_b]
    out_shape = [jax.ShapeDtypeStruct((m, n), BF16)]
    out_specs = [pl.BlockSpec((tm, n), lambda i: (i, 0))]
    if shared_kv:
        nb = tm // MOBA_BLOCK
        in_specs.append(_resident((ATTN_DIM, D_MODEL)))
        args.append(values_t_w)
        out_shape += [jax.ShapeDtypeStruct((m // MOBA_BLOCK, 1, n), F32),
                      jax.ShapeDtypeStruct((m // MOBA_BLOCK, ATTN_DIM, MOBA_BLOCK), BF16)]
        out_specs += [pl.BlockSpec((nb, 1, n), lambda i: (i, 0, 0)),
                      pl.BlockSpec((nb, ATTN_DIM, MOBA_BLOCK), lambda i: (i, 0, 0))]
    if class_major:
        in_specs.append(_resident((tm, tm)))
        args.append(class_perm)
        out_shape = [jax.ShapeDtypeStruct((m // seq, A_CLASSES, seq // A_CLASSES, n), BF16)]
        out_specs = [pl.BlockSpec((None, A_CLASSES, tm // A_CLASSES, n),
                                  lambda i: (i // tiles_per_seq, 0, i % tiles_per_seq, 0))]
    return pl.pallas_call(
        functools.partial(_proj_body, groups_a=groups_a, groups_b=groups_b, shared_kv=shared_kv,
                          class_major=class_major),
        grid=(m // tm,),
        in_specs=in_specs,
        out_specs=out_specs,
        out_shape=out_shape,
        compiler_params=pltpu.CompilerParams(
            dimension_semantics=("parallel",), vmem_limit_bytes=VMEM_LIMIT),
        name="proj",
    )(*args)


def _first_head(rows):
    return lax.broadcasted_iota(jnp.int32, (rows, LANES), 1) < HEAD_DIM


def _band_scores(q, k):
    fq = _first_head(q.shape[0])
    zero = jnp.zeros_like(q)
    q_both = jnp.concatenate([jnp.where(fq, q, zero), jnp.where(fq, zero, q)], axis=0)
    return lax.dot_general(q_both, k, (((1,), (1,)), ((), ())), preferred_element_type=F32)


def _band_softmax(s, v, bias):
    nq, nk = bias.shape
    fq, fk = _first_head(nq), _first_head(nk)
    one = jnp.ones_like(v)
    outs, tops = [], []
    for head in range(2):
        sh = s[head * nq:(head + 1) * nq] + bias
        top = jnp.max(sh, axis=-1, keepdims=True)
        p = jnp.exp(sh - top).astype(BF16)
        vh = jnp.where(fk, v, one) if head == 0 else jnp.where(fk, one, v)
        outs.append(jnp.dot(p, vh, preferred_element_type=F32))
        tops.append(top)
    acc = jnp.where(fq, outs[0], outs[1])
    den = pltpu.roll(jnp.where(fq, outs[1], outs[0]), HEAD_DIM, 1)
    return acc, jnp.where(fq, tops[0], tops[1]), den


def _merge_softmax(acc_a, top_a, den_a, acc_b, top_b, den_b):
    top = jnp.maximum(top_a, top_b)
    wa, wb = jnp.exp(top_a - top), jnp.exp(top_b - top)
    return wa * acc_a + wb * acc_b, top, wa * den_a + wb * den_b


def _attn_a_body(q_ref, k_ref, v_ref, b16_ref, b4_ref, b1_ref, o_ref, acc_ref, top_ref, den_ref):
    n_idx = q_ref.shape[1]

    def class_tiles(r0, carry):
        classes = [4 * r0 + u for u in range(4)]
        scores = [_band_scores(q_ref[r], k_ref[r]) for r in classes]
        for r, s in zip(classes, scores):
            acc_ref[r], top_ref[r], den_ref[r] = _band_softmax(s, v_ref[r], b16_ref[...])
        return carry

    lax.fori_loop(0, A_CLASSES // 4, class_tiles, 0)

    def tiles_d4(it, carry):
        i0 = pl.multiple_of(it * 32, 32)
        j0 = pl.multiple_of(jnp.maximum(it - 1, 0) * 32, 32)
        bias = b4_ref[jnp.minimum(it, 1)]
        scores = []
        for r4 in range(4):
            classes = [r4 + 4 * c for c in range(4)]
            q = jnp.concatenate([q_ref[r, pl.ds(i0, 32), :] for r in classes], axis=0)
            k = jnp.concatenate([k_ref[r, pl.ds(j0, 64), :] for r in classes], axis=0)
            scores.append(_band_scores(q, k))
        for r4 in range(4):
            classes = [r4 + 4 * c for c in range(4)]
            v = jnp.concatenate([v_ref[r, pl.ds(j0, 64), :] for r in classes], axis=0)
            acc, top, den = _band_softmax(scores[r4], v, bias)
            for c, r in enumerate(classes):
                at = (r, pl.ds(i0, 32), slice(None))
                part = slice(c * 32, (c + 1) * 32)
                acc_ref[at], top_ref[at], den_ref[at] = _merge_softmax(
                    acc_ref[at], top_ref[at], den_ref[at], acc[part], top[part], den[part])
        return carry

    lax.fori_loop(0, n_idx // 32, tiles_d4, 0)

    def scores_d1(it):
        i0 = pl.multiple_of(it * 16, 16)
        j0 = pl.multiple_of(jnp.maximum(it - 1, 0) * 16, 16)
        q = jnp.concatenate([q_ref[r, pl.ds(i0, 16), :] for r in range(A_CLASSES)], axis=0)
        k = jnp.concatenate([k_ref[r, pl.ds(j0, 32), :] for r in range(A_CLASSES)], axis=0)
        return _band_scores(q, k)

    def finish_d1(it, first, s):
        i0 = pl.multiple_of(it * 16, 16)
        j0 = pl.multiple_of(jnp.maximum(it - 1, 0) * 16, 16)
        v = jnp.concatenate([v_ref[r, pl.ds(j0, 32), :] for r in range(A_CLASSES)], axis=0)
        acc, top, den = _band_softmax(s, v, b1_ref[first])
        for r in range(A_CLASSES):
            at = (r, pl.ds(i0, 16), slice(None))
            part = slice(r * 16, (r + 1) * 16)
            acc_m, _, den_m = _merge_softmax(
                acc_ref[at], top_ref[at], den_ref[at], acc[part], top[part], den[part])
            o_ref[at] = (acc_m / den_m).astype(BF16)

    half = n_idx // 32

    def tiles_d1(it, carry):
        s_lo, s_hi = scores_d1(it), scores_d1(it + half)
        finish_d1(it, jnp.minimum(it, 1), s_lo)
        finish_d1(it + half, 1, s_hi)
        return carry

    lax.fori_loop(0, half, tiles_d1, 0)


def _band_bias(n_q_classes, q_len, k_len, dilation_step, offset):
    cq = jnp.arange(n_q_classes * q_len, dtype=jnp.int32) // q_len
    iq = jnp.arange(n_q_classes * q_len, dtype=jnp.int32) % q_len
    ck = jnp.arange(n_q_classes * k_len, dtype=jnp.int32) // k_len
    ik = jnp.arange(n_q_classes * k_len, dtype=jnp.int32) % k_len
    dist = dilation_step * (offset + iq[:, None] - ik[None, :]) + (cq[:, None] - ck[None, :])
    return jnp.where((dist >= 0) & (dist <= BAND_BLOCK), 0.0, MASKED_SCORE).astype(F32)


def _attn_a(qkv, batch, seq):
    n_idx = seq // A_CLASSES
    b16 = _band_bias(1, n_idx, n_idx, 1, 0)
    b4 = jnp.stack([_band_bias(4, 32, 64, 4, off) for off in (0, 32)])
    b1 = jnp.stack([_band_bias(A_CLASSES, 16, 32, A_CLASSES, off) for off in (0, 16)])

    def spec(which):
        return pl.BlockSpec((None, A_CLASSES, n_idx, LANES),
                            lambda b, hp: (b, 0, 0, which * N_HEAD_PAIRS + hp))

    return pl.pallas_call(
        _attn_a_body,
        grid=(batch, N_HEAD_PAIRS),
        in_specs=[spec(0), spec(1), spec(2),
                  _resident(b16.shape), _resident(b4.shape), _resident(b1.shape)],
        out_specs=pl.BlockSpec((None, A_CLASSES, n_idx, LANES), lambda b, hp: (b, 0, 0, hp)),
        out_shape=jax.ShapeDtypeStruct((batch, A_CLASSES, n_idx, ATTN_DIM), BF16),
        scratch_shapes=[pltpu.VMEM((A_CLASSES, n_idx, LANES), F32)] * 3,
        compiler_params=pltpu.CompilerParams(
            dimension_semantics=("parallel", "parallel"), vmem_limit_bytes=VMEM_LIMIT),
        name="attn_a",
    )(qkv, qkv, qkv, b16, b4, b1)


def _class_perm(tm):
    per = tm // A_CLASSES
    dst = jnp.arange(tm, dtype=jnp.int32)
    src = A_CLASSES * (dst % per) + dst // per
    return (src[:, None] == jnp.arange(tm, dtype=jnp.int32)[None, :]).astype(BF16)


def _wo_a_body(h_ref, o_ref, unperm_ref, w_ref, out_ref):
    tm = h_ref.shape[0]
    o = o_ref[...].reshape(tm, ATTN_DIM)
    o = jnp.dot(unperm_ref[...], o, preferred_element_type=F32).astype(BF16)
    out_ref[...] = h_ref[...] + jnp.dot(o, w_ref[...], preferred_element_type=F32)


def _wo_a(h, o, unperm, w, *, tm=TOKEN_TILE):
    m = h.shape[0]
    tiles_per_seq = o.shape[2] * A_CLASSES // tm
    row = pl.BlockSpec((tm, D_MODEL), lambda i: (i, 0))
    return pl.pallas_call(
        _wo_a_body,
        grid=(m // tm,),
        in_specs=[row,
                  pl.BlockSpec((None, A_CLASSES, tm // A_CLASSES, ATTN_DIM),
                               lambda i: (i // tiles_per_seq, 0, i % tiles_per_seq, 0)),
                  _resident((tm, tm)), _resident((ATTN_DIM, D_MODEL))],
        out_specs=row,
        out_shape=jax.ShapeDtypeStruct((m, D_MODEL), F32),
        compiler_params=pltpu.CompilerParams(
            dimension_semantics=("parallel",), vmem_limit_bytes=VMEM_LIMIT),
        name="wo_a",
    )(h, o, unperm, w)


MOBA_GROUP = 2
ONES_ROWS = 16
MOBA_PAIRS_PER_STEP = 2


def _moba_body(q_ref, k_ref, vt_ref, tag_ref, km_ref, o_ref, sa_ref, sb_ref):
    own = pl.program_id(2)
    rows = q_ref.shape[0]
    n_blocks = km_ref.shape[0]
    cols = 2 * rows
    group_keys = MOBA_GROUP * MOBA_BLOCK
    contract_last = (((1,), (1,)), ((), ()))
    first_head = _first_head(rows)
    blk = lax.broadcasted_iota(jnp.int32, (n_blocks, cols), 0)
    key_pos = lax.broadcasted_iota(jnp.int32, (MOBA_BLOCK, cols), 0)
    query_pos = lax.broadcasted_iota(jnp.int32, (MOBA_BLOCK, cols), 1) & (rows - 1)
    own0 = pl.multiple_of(own * MOBA_BLOCK, MOBA_BLOCK)

    def values_t(n0, n, pair, head):
        lo = pair * LANES + head * HEAD_DIM
        vt = jnp.concatenate([vt_ref[n0 + u, lo:lo + HEAD_DIM, :] for u in range(n)], axis=1)
        return jnp.concatenate([vt, jnp.ones((ONES_ROWS, n * MOBA_BLOCK), BF16)], axis=0)

    def weighted_values(n0, n, pair, p):
        return [jnp.dot(values_t(n0, n, pair, head), p[:, head * rows:(head + 1) * rows],
                        preferred_element_type=F32) for head in range(2)]

    q_augs, init = [], []
    for pair in range(MOBA_PAIRS_PER_STEP):
        lanes = slice(pair * LANES, (pair + 1) * LANES)
        q = q_ref[:, lanes]
        zero = jnp.zeros_like(q)
        q_both = jnp.concatenate([jnp.where(first_head, q, zero), jnp.where(first_head, zero, q)], axis=0)

        gate = lax.dot_general(km_ref[:, lanes].astype(BF16), q_both, contract_last,
                               preferred_element_type=F32)
        gate = jnp.where(blk < own, gate, -jnp.inf)
        chosen = jnp.zeros(gate.shape, jnp.bool_)
        for _ in range(MOBA_TOPK):
            best = jnp.max(gate, axis=0, keepdims=True)
            at = jnp.min(jnp.where(gate == best, blk, n_blocks), axis=0, keepdims=True)
            hit = blk == at
            chosen = chosen | (hit & (best > -jnp.inf))
            gate = jnp.where(hit, -jnp.inf, gate)
        bias_t = jnp.concatenate([jnp.where(chosen, 0.0, MASKED_SCORE),
                                  jnp.full((LANES - n_blocks, cols), MASKED_SCORE, F32)], axis=0)
        q_augs.append(jnp.concatenate([q_both, bias_t.T.astype(BF16)], axis=1))

        s = lax.dot_general(k_ref[pl.ds(own0, MOBA_BLOCK), lanes], q_both, contract_last,
                            preferred_element_type=F32)
        s = jnp.where(key_pos <= query_pos, s, MASKED_SCORE)
        m0 = jnp.max(s, axis=0, keepdims=True)
        p = jnp.exp(s - m0).astype(BF16)
        acc0 = weighted_values(own, 1, pair, p)
        init.append((m0, acc0, jnp.full_like(m0, MASKED_SCORE), [jnp.zeros_like(a) for a in acc0]))

    def scores(n0, pair, s_ref):
        k0 = pl.multiple_of(n0 * MOBA_BLOCK, group_keys)
        k_aug = jnp.concatenate([k_ref[pl.ds(k0, group_keys), pair * LANES:(pair + 1) * LANES],
                                 tag_ref[pl.ds(k0, group_keys), :]], axis=1)
        s = lax.dot_general(k_aug, q_augs[pair], contract_last, preferred_element_type=F32)
        s_ref[pair] = s
        return jnp.max(s, axis=0, keepdims=True)

    def absorb(n0, pair, s_ref, s_max, m, acc):
        m_new = jnp.maximum(m, s_max)
        p = jnp.exp(s_ref[pair] - m_new).astype(BF16)
        alpha = jnp.exp(m - m_new)
        pv = weighted_values(n0, MOBA_GROUP, pair, p)
        return m_new, [alpha[:, head * rows:(head + 1) * rows] * acc[head] + pv[head] for head in range(2)]

    def past_pair(j, carry):
        n0 = 2 * MOBA_GROUP * j
        n_next = jnp.minimum(n0 + 2 * MOBA_GROUP, n_blocks - MOBA_GROUP)
        out = []
        for pair, (m_a, acc_a, m_b, acc_b, max_a) in enumerate(carry):
            max_b = scores(n0 + MOBA_GROUP, pair, sb_ref)
            m_a, acc_a = absorb(n0, pair, sa_ref, max_a, m_a, acc_a)
            max_a = scores(n_next, pair, sa_ref)
            m_b, acc_b = absorb(n0 + MOBA_GROUP, pair, sb_ref, max_b, m_b, acc_b)
            out.append((m_a, acc_a, m_b, acc_b, max_a))
        return out

    n_pairs = (own + 2 * MOBA_GROUP - 1) // (2 * MOBA_GROUP)
    init = [state + (scores(0, pair, sa_ref),) for pair, state in enumerate(init)]
    final = lax.fori_loop(0, n_pairs, past_pair, init)
    for pair, (m_a, acc_a, m_b, acc_b, _) in enumerate(final):
        top = jnp.maximum(m_a, m_b)
        wa, wb = jnp.exp(m_a - top), jnp.exp(m_b - top)
        out_t = []
        for head in range(2):
            part = slice(head * rows, (head + 1) * rows)
            acc = wa[:, part] * acc_a[head] + wb[:, part] * acc_b[head]
            out_t.append(acc[:HEAD_DIM] * (1.0 / acc[HEAD_DIM:HEAD_DIM + 1]))
        o_ref[:, pair * LANES:(pair + 1) * LANES] = jnp.concatenate(out_t, axis=0).T.astype(BF16)


def _moba(q, k, vt, key_means, batch, seq):
    n_blocks = seq // MOBA_BLOCK
    q3 = q.reshape(batch, seq, ATTN_DIM)
    k3 = k.reshape(batch, seq, ATTN_DIM)
    km3 = key_means.reshape(batch, n_blocks, ATTN_DIM)
    tag = (jnp.arange(seq, dtype=jnp.int32)[:, None] // MOBA_BLOCK
           == jnp.arange(LANES, dtype=jnp.int32)[None, :]).astype(BF16)
    width = MOBA_PAIRS_PER_STEP * LANES
    o = pl.pallas_call(
        _moba_body,
        grid=(batch, ATTN_DIM // width, n_blocks),
        in_specs=[pl.BlockSpec((None, MOBA_BLOCK, width), lambda b, hp, i: (b, i, hp)),
                  pl.BlockSpec((None, seq, width), lambda b, hp, i: (b, 0, hp)),
                  pl.BlockSpec((n_blocks, width, MOBA_BLOCK), lambda b, hp, i: (b, hp, 0)),
                  _resident((seq, LANES)),
                  pl.BlockSpec((None, n_blocks, width), lambda b, hp, i: (b, 0, hp))],
        out_specs=pl.BlockSpec((None, MOBA_BLOCK, width), lambda b, hp, i: (b, i, hp)),
        out_shape=jax.ShapeDtypeStruct((batch, seq, ATTN_DIM), BF16),
        scratch_shapes=[pltpu.VMEM((MOBA_PAIRS_PER_STEP, MOBA_GROUP * MOBA_BLOCK, 2 * MOBA_BLOCK), F32)] * 2,
        compiler_params=pltpu.CompilerParams(
            dimension_semantics=("parallel", "parallel", "arbitrary"), vmem_limit_bytes=VMEM_LIMIT),
        name="moba",
    )(q3, k3, vt, tag, km3)
    return o.reshape(batch * seq, ATTN_DIM)


def _wo_b_body(h_ref, o_ref, w_ref, out_ref):
    out_ref[...] = h_ref[...] + jnp.dot(o_ref[...], w_ref[...], preferred_element_type=F32)


def _wo_b(h, o, w, *, tm=512):
    m = h.shape[0]
    row = pl.BlockSpec((tm, D_MODEL), lambda i: (i, 0))
    return pl.pallas_call(
        _wo_b_body,
        grid=(m // tm,),
        in_specs=[row, row, _resident((ATTN_DIM, D_MODEL))],
        out_specs=row,
        out_shape=jax.ShapeDtypeStruct((m, D_MODEL), F32),
        compiler_params=pltpu.CompilerParams(
            dimension_semantics=("parallel",), vmem_limit_bytes=VMEM_LIMIT),
        name="wo_b",
    )(h, o, w)


def kernel(x, ffn1_norm, ffn1_w_gate, ffn1_w_up, ffn1_w_down, mix_norm, ffn2_norm, ffn2_w_gate,
           ffn2_w_up, ffn2_w_down, a_w_qkv, a_w_o, kv_norm, kv_w, b_w_q, b_w_o, final_norm):
    batch, seq, _ = x.shape
    bf = lambda w: w.astype(BF16)
    rot_q = _rotary_tables(seq, HEAD_DIM ** -0.5)
    rot_k = _rotary_tables(seq, 1.0)
    h = x.reshape(batch * seq, D_MODEL)

    h = _ffn(h, ffn1_norm[0], bf(ffn1_w_gate[0]), bf(ffn1_w_up[0]), bf(ffn1_w_down[0]))
    perm = _class_perm(TOKEN_TILE)
    per = TOKEN_TILE // A_CLASSES
    order = jnp.arange(seq, dtype=jnp.int32).reshape(seq // TOKEN_TILE, per, A_CLASSES)
    order = order.transpose(0, 2, 1).reshape(seq)
    qkv = _proj(h, mix_norm[0], bf(a_w_qkv[0]), rot_q[:, order], rot_k[:, order],
                groups_a=N_HEAD_PAIRS, groups_b=N_HEAD_PAIRS, class_perm=perm)[0]
    h = _wo_a(h, _attn_a(qkv, batch, seq), perm.T, bf(a_w_o[0]))
    h = _ffn(h, ffn2_norm[0], bf(ffn2_w_gate[0]), bf(ffn2_w_up[0]), bf(ffn2_w_down[0]))

    k, key_means, vt = _proj(h, kv_norm, bf(kv_w[:, :ATTN_DIM]), rot_k, rot_k, groups_a=0,
                             groups_b=N_HEAD_PAIRS, values_t_w=bf(kv_w[:, ATTN_DIM:].T))

    h = _ffn(h, ffn1_norm[1], bf(ffn1_w_gate[1]), bf(ffn1_w_up[1]), bf(ffn1_w_down[1]))
    q = _proj(h, mix_norm[1], bf(b_w_q[0]), rot_q, rot_q, groups_a=N_HEAD_PAIRS, groups_b=0)[0]
    h = _wo_b(h, _moba(q, k, vt, key_means, batch, seq), bf(b_w_o[0]))
    h = _ffn(h, ffn2_norm[1], bf(ffn2_w_gate[1]), bf(ffn2_w_up[1]), bf(ffn2_w_down[1]),
             final_g=final_norm)
    return h.reshape(batch, seq, D_MODEL)
```

```python
import functools

import jax
import jax.numpy as jnp
from jax import lax
from jax.experimental import pallas as pl
from jax.experimental.pallas import tpu as pltpu

D_MODEL = 1024
N_HEADS = 16
HEAD_DIM = 64
ATTN_DIM = N_HEADS * HEAD_DIM
ROT_DIM = HEAD_DIM // 4
ROPE_THETA = 500000.0
D_FF = 2816
FFN_RESIDUAL_WEIGHT = 0.5
DILATION_PAIRS = ((128, 1), (512, 4), (2048, 16))
MOBA_BLOCK = 256
MOBA_TOPK = 3
RMS_EPS = 1e-6

LANES = 128
N_HEAD_PAIRS = ATTN_DIM // LANES
BAND_BLOCK = 128
A_CLASSES = 16
TOKEN_TILE = 512
MASKED_SCORE = -1e30
QUERY_SCALE = HEAD_DIM ** -0.5 * 1.4426950408889634
VMEM_LIMIT = 56 * 1024 * 1024

F32 = jnp.float32
BF16 = jnp.bfloat16


def _rms(x, g):
    return x * lax.rsqrt(jnp.mean(x * x, axis=-1, keepdims=True) + RMS_EPS) * g


def _resident(shape):
    zeros = (0,) * len(shape)
    return pl.BlockSpec(shape, lambda *_: zeros, pipeline_mode=pl.Buffered(1))


def _ffn_body(x_ref, g_ref, wg_ref, wu_ref, wd_ref, *rest, final_norm):
    if final_norm:
        fg_ref, o_ref = rest
    else:
        (o_ref,) = rest
    x = x_ref[...]
    hn = _rms(x, g_ref[...]).astype(BF16)
    gate = jnp.dot(hn, wg_ref[...], preferred_element_type=F32)
    up = jnp.dot(hn, wu_ref[...], preferred_element_type=F32)
    act = (jax.nn.silu(gate) * up).astype(BF16)
    y = x + FFN_RESIDUAL_WEIGHT * jnp.dot(act, wd_ref[...], preferred_element_type=F32)
    if final_norm:
        y = _rms(y, fg_ref[...])
    o_ref[...] = y


def _ffn(h, g, wg, wu, wd, final_g=None, *, tm=512):
    m = h.shape[0]
    row = pl.BlockSpec((tm, D_MODEL), lambda i: (i, 0))
    in_specs = [row, _resident((1, D_MODEL)), _resident((D_MODEL, D_FF)),
                _resident((D_MODEL, D_FF)), _resident((D_FF, D_MODEL))]
    args = [h, g.reshape(1, D_MODEL), wg, wu, wd]
    if final_g is not None:
        in_specs.append(_resident((1, D_MODEL)))
        args.append(final_g.reshape(1, D_MODEL))
    return pl.pallas_call(
        functools.partial(_ffn_body, final_norm=final_g is not None),
        grid=(m // tm,),
        in_specs=in_specs,
        out_specs=row,
        out_shape=jax.ShapeDtypeStruct((m, D_MODEL), F32),
        compiler_params=pltpu.CompilerParams(
            dimension_semantics=("parallel",), vmem_limit_bytes=VMEM_LIMIT),
        name="ffn",
    )(*args)


def _rotary_tables(seq, scale):
    half = ROT_DIM // 2
    inv_freq = ROPE_THETA ** (-jnp.arange(half, dtype=F32) / half)
    ang = jnp.arange(seq, dtype=F32)[:, None] * inv_freq[None, :]
    cos, sin = jnp.cos(ang), jnp.sin(ang)
    zeros = jnp.zeros((seq, HEAD_DIM - ROT_DIM), F32)
    z8 = jnp.zeros((seq, half), F32)
    c = jnp.concatenate([cos, cos, jnp.ones_like(zeros)], axis=1)
    s_up = jnp.concatenate([z8, sin, zeros], axis=1)
    s_dn = jnp.concatenate([-sin, z8, zeros], axis=1)
    tab = jnp.stack([c, s_up, s_dn], axis=0) * scale
    return jnp.tile(tab, (1, 1, LANES // HEAD_DIM))


def _proj_body(*refs, groups_a, groups_b, shared_kv, class_major):
    if shared_kv:
        x_ref, g_ref, w_ref, rot_a_ref, rot_b_ref, wvt_ref, o_ref, km_ref, vt_ref = refs
    elif class_major:
        x_ref, g_ref, w_ref, rot_a_ref, rot_b_ref, perm_ref, o_ref = refs
    else:
        x_ref, g_ref, w_ref, rot_a_ref, rot_b_ref, o_ref = refs
    x = x_ref[...]
    hn = _rms(x, g_ref[...]).astype(BF16)
    if class_major:
        hn = jnp.dot(perm_ref[...], hn, preferred_element_type=F32).astype(BF16)
    y = jnp.dot(hn, w_ref[...], preferred_element_type=F32)
    tm, n = y.shape
    for j in range(n // LANES):
        t = y[:, j * LANES:(j + 1) * LANES]
        rot_ref = rot_a_ref if j < groups_a else (rot_b_ref if j < groups_a + groups_b else None)
        if rot_ref is not None:
            t = (t * rot_ref[0]
                 + pltpu.roll(t, ROT_DIM // 2, 1) * rot_ref[1]
                 + pltpu.roll(t, LANES - ROT_DIM // 2, 1) * rot_ref[2])
            if shared_kv:
                km_ref[:, 0, j * LANES:(j + 1) * LANES] = jnp.mean(
                    t.reshape(tm // MOBA_BLOCK, MOBA_BLOCK, LANES), axis=1)
        if class_major:
            o_ref[:, :, j * LANES:(j + 1) * LANES] = t.reshape(
                A_CLASSES, tm // A_CLASSES, LANES).astype(BF16)
        else:
            o_ref[:, j * LANES:(j + 1) * LANES] = t.astype(BF16)
    if shared_kv:
        vt = lax.dot_general(wvt_ref[...], hn, (((1,), (1,)), ((), ())), preferred_element_type=F32)
        for j in range(tm // MOBA_BLOCK):
            vt_ref[j] = vt[:, j * MOBA_BLOCK:(j + 1) * MOBA_BLOCK].astype(BF16)


def _proj(h, g, w, rot_a, rot_b, *, groups_a, groups_b, values_t_w=None, class_perm=None, tm=TOKEN_TILE):
    m = h.shape[0]
    n = w.shape[1]
    seq = rot_a.shape[1]
    tiles_per_seq = seq // tm
    shared_kv = values_t_w is not None
    class_major = class_perm is not None
    rot_spec = pl.BlockSpec((3, tm, LANES), lambda i: (0, i % tiles_per_seq, 0))
    in_specs = [pl.BlockSpec((tm, D_MODEL), lambda i: (i, 0)), _resident((1, D_MODEL)),
                _resident((D_MODEL, n)), rot_spec, rot_spec]
    args = [h, g.reshape(1, D_MODEL), w, rot_a, rot_b]
    out_shape = [jax.ShapeDtypeStruct((m, n), BF16)]
    out_specs = [pl.BlockSpec((tm, n), lambda i: (i, 0))]
    if shared_kv:
        nb = tm // MOBA_BLOCK
        in_specs.append(_resident((ATTN_DIM, D_MODEL)))
        args.append(values_t_w)
        out_shape += [jax.ShapeDtypeStruct((m // MOBA_BLOCK, 1, n), F32),
                      jax.ShapeDtypeStruct((m // MOBA_BLOCK, ATTN_DIM, MOBA_BLOCK), BF16)]
        out_specs += [pl.BlockSpec((nb, 1, n), lambda i: (i, 0, 0)),
                      pl.BlockSpec((nb, ATTN_DIM, MOBA_BLOCK), lambda i: (i, 0, 0))]
    if class_major:
        in_specs.append(_resident((tm, tm)))
        args.append(class_perm)
        out_shape = [jax.ShapeDtypeStruct((m // seq, A_CLASSES, seq // A_CLASSES, n), BF16)]
        out_specs = [pl.BlockSpec((None, A_CLASSES, tm // A_CLASSES, n),
                                  lambda i: (i // tiles_per_seq, 0, i % tiles_per_seq, 0))]
    return pl.pallas_call(
        functools.partial(_proj_body, groups_a=groups_a, groups_b=groups_b, shared_kv=shared_kv,
                          class_major=class_major),
        grid=(m // tm,),
        in_specs=in_specs,
        out_specs=out_specs,
        out_shape=out_shape,
        compiler_params=pltpu.CompilerParams(
            dimension_semantics=("parallel",), vmem_limit_bytes=VMEM_LIMIT),
        name="proj",
    )(*args)


def _first_head(rows):
    return lax.broadcasted_iota(jnp.int32, (rows, LANES), 1) < HEAD_DIM


def _band_scores(q, k):
    fq = _first_head(q.shape[0])
    zero = jnp.zeros_like(q)
    q_both = jnp.concatenate([jnp.where(fq, q, zero), jnp.where(fq, zero, q)], axis=0)
    return lax.dot_general(q_both, k, (((1,), (1,)), ((), ())), preferred_element_type=F32)


def _band_softmax(s, v, bias):
    nq, nk = bias.shape
    fq, fk = _first_head(nq), _first_head(nk)
    one = jnp.ones_like(v)
    outs, tops = [], []
    for head in range(2):
        sh = s[head * nq:(head + 1) * nq] + bias
        top = jnp.max(sh, axis=-1, keepdims=True)
        p = jnp.exp2(sh - top).astype(BF16)
        vh = jnp.where(fk, v, one) if head == 0 else jnp.where(fk, one, v)
        outs.append(jnp.dot(p, vh, preferred_element_type=F32))
        tops.append(top)
    acc = jnp.where(fq, outs[0], outs[1])
    den = pltpu.roll(jnp.where(fq, outs[1], outs[0]), HEAD_DIM, 1)
    return acc, jnp.where(fq, tops[0], tops[1]), den


def _merge_softmax(acc_a, top_a, den_a, acc_b, top_b, den_b):
    top = jnp.maximum(top_a, top_b)
    wa, wb = jnp.exp2(top_a - top), jnp.exp2(top_b - top)
    return wa * acc_a + wb * acc_b, top, wa * den_a + wb * den_b


def _attn_a_body(q_ref, k_ref, v_ref, b16_ref, b4_ref, b1_ref, o_ref, acc_ref, top_ref, den_ref):
    n_idx = q_ref.shape[1]

    def class_tiles(r0, carry):
        classes = [4 * r0 + u for u in range(4)]
        scores = [_band_scores(q_ref[r], k_ref[r]) for r in classes]
        for r, s in zip(classes, scores):
            acc_ref[r], top_ref[r], den_ref[r] = _band_softmax(s, v_ref[r], b16_ref[...])
        return carry

    lax.fori_loop(0, A_CLASSES // 4, class_tiles, 0)

    def tiles_d4(it_lo, carry):
        tiles = [(it, r4) for it in (it_lo, it_lo + n_idx // 64) for r4 in range(4)]

        def window(it, r4):
            i0 = pl.multiple_of(it * 32, 32)
            j0 = pl.multiple_of(jnp.maximum(it - 1, 0) * 32, 32)
            return i0, j0, [r4 + 4 * c for c in range(4)]

        scores = []
        for it, r4 in tiles:
            i0, j0, classes = window(it, r4)
            q = jnp.concatenate([q_ref[r, pl.ds(i0, 32), :] for r in classes], axis=0)
            k = jnp.concatenate([k_ref[r, pl.ds(j0, 64), :] for r in classes], axis=0)
            scores.append(_band_scores(q, k))
        for (it, r4), s in zip(tiles, scores):
            i0, j0, classes = window(it, r4)
            v = jnp.concatenate([v_ref[r, pl.ds(j0, 64), :] for r in classes], axis=0)
            acc, top, den = _band_softmax(s, v, b4_ref[jnp.minimum(it, 1)])
            for c, r in enumerate(classes):
                at = (r, pl.ds(i0, 32), slice(None))
                part = slice(c * 32, (c + 1) * 32)
                acc_ref[at], top_ref[at], den_ref[at] = _merge_softmax(
                    acc_ref[at], top_ref[at], den_ref[at], acc[part], top[part], den[part])
        return carry

    lax.fori_loop(0, n_idx // 64, tiles_d4, 0)

    def scores_d1(it):
        i0 = pl.multiple_of(it * 16, 16)
        j0 = pl.multiple_of(jnp.maximum(it - 1, 0) * 16, 16)
        q = jnp.concatenate([q_ref[r, pl.ds(i0, 16), :] for r in range(A_CLASSES)], axis=0)
        k = jnp.concatenate([k_ref[r, pl.ds(j0, 32), :] for r in range(A_CLASSES)], axis=0)
        return _band_scores(q, k)

    def finish_d1(it, first, s):
        i0 = pl.multiple_of(it * 16, 16)
        j0 = pl.multiple_of(jnp.maximum(it - 1, 0) * 16, 16)
        v = jnp.concatenate([v_ref[r, pl.ds(j0, 32), :] for r in range(A_CLASSES)], axis=0)
        acc, top, den = _band_softmax(s, v, b1_ref[first])
        for r in range(A_CLASSES):
            at = (r, pl.ds(i0, 16), slice(None))
            part = slice(r * 16, (r + 1) * 16)
            acc_m, _, den_m = _merge_softmax(
                acc_ref[at], top_ref[at], den_ref[at], acc[part], top[part], den[part])
            o_ref[at] = (acc_m / den_m).astype(BF16)

    half = n_idx // 32

    def tiles_d1(it, carry):
        s_lo, s_hi = scores_d1(it), scores_d1(it + half)
        finish_d1(it, jnp.minimum(it, 1), s_lo)
        finish_d1(it + half, 1, s_hi)
        return carry

    lax.fori_loop(0, half, tiles_d1, 0)


def _band_bias(n_q_classes, q_len, k_len, dilation_step, offset):
    cq = jnp.arange(n_q_classes * q_len, dtype=jnp.int32) // q_len
    iq = jnp.arange(n_q_classes * q_len, dtype=jnp.int32) % q_len
    ck = jnp.arange(n_q_classes * k_len, dtype=jnp.int32) // k_len
    ik = jnp.arange(n_q_classes * k_len, dtype=jnp.int32) % k_len
    dist = dilation_step * (offset + iq[:, None] - ik[None, :]) + (cq[:, None] - ck[None, :])
    return jnp.where((dist >= 0) & (dist <= BAND_BLOCK), 0.0, MASKED_SCORE).astype(F32)


def _attn_a(qkv, batch, seq):
    n_idx = seq // A_CLASSES
    b16 = _band_bias(1, n_idx, n_idx, 1, 0)
    b4 = jnp.stack([_band_bias(4, 32, 64, 4, off) for off in (0, 32)])
    b1 = jnp.stack([_band_bias(A_CLASSES, 16, 32, A_CLASSES, off) for off in (0, 16)])

    def spec(which):
        return pl.BlockSpec((None, A_CLASSES, n_idx, LANES),
                            lambda b, hp: (b, 0, 0, which * N_HEAD_PAIRS + hp))

    return pl.pallas_call(
        _attn_a_body,
        grid=(batch, N_HEAD_PAIRS),
        in_specs=[spec(0), spec(1), spec(2),
                  _resident(b16.shape), _resident(b4.shape), _resident(b1.shape)],
        out_specs=pl.BlockSpec((None, A_CLASSES, n_idx, LANES), lambda b, hp: (b, 0, 0, hp)),
        out_shape=jax.ShapeDtypeStruct((batch, A_CLASSES, n_idx, ATTN_DIM), BF16),
        scratch_shapes=[pltpu.VMEM((A_CLASSES, n_idx, LANES), F32)] * 3,
        compiler_params=pltpu.CompilerParams(
            dimension_semantics=("parallel", "parallel"), vmem_limit_bytes=VMEM_LIMIT),
        name="attn_a",
    )(qkv, qkv, qkv, b16, b4, b1)


def _class_perm(tm):
    per = tm // A_CLASSES
    dst = jnp.arange(tm, dtype=jnp.int32)
    src = A_CLASSES * (dst % per) + dst // per
    return (src[:, None] == jnp.arange(tm, dtype=jnp.int32)[None, :]).astype(BF16)


def _wo_a_body(h_ref, o_ref, unperm_ref, w_ref, out_ref):
    tm = h_ref.shape[0]
    o = o_ref[...].reshape(tm, ATTN_DIM)
    o = jnp.dot(unperm_ref[...], o, preferred_element_type=F32).astype(BF16)
    out_ref[...] = h_ref[...] + jnp.dot(o, w_ref[...], preferred_element_type=F32)


def _wo_a(h, o, unperm, w, *, tm=TOKEN_TILE):
    m = h.shape[0]
    tiles_per_seq = o.shape[2] * A_CLASSES // tm
    row = pl.BlockSpec((tm, D_MODEL), lambda i: (i, 0))
    return pl.pallas_call(
        _wo_a_body,
        grid=(m // tm,),
        in_specs=[row,
                  pl.BlockSpec((None, A_CLASSES, tm // A_CLASSES, ATTN_DIM),
                               lambda i: (i // tiles_per_seq, 0, i % tiles_per_seq, 0)),
                  _resident((tm, tm)), _resident((ATTN_DIM, D_MODEL))],
        out_specs=row,
        out_shape=jax.ShapeDtypeStruct((m, D_MODEL), F32),
        compiler_params=pltpu.CompilerParams(
            dimension_semantics=("parallel",), vmem_limit_bytes=VMEM_LIMIT),
        name="wo_a",
    )(h, o, unperm, w)


MOBA_GROUP = 4
ONES_ROWS = 16
MOBA_PAIRS_PER_STEP = 4


def _moba_body(q_ref, k_ref, vt_ref, tag_ref, km_ref, o_ref, s_ref):
    own = pl.program_id(2)
    rows = q_ref.shape[0]
    n_blocks = km_ref.shape[0]
    cols = 2 * rows
    group_keys = MOBA_GROUP * MOBA_BLOCK
    contract_last = (((1,), (1,)), ((), ()))
    first_head = _first_head(rows)
    blk = lax.broadcasted_iota(jnp.int32, (n_blocks, cols), 0)
    key_pos = lax.broadcasted_iota(jnp.int32, (MOBA_BLOCK, cols), 0)
    query_pos = lax.broadcasted_iota(jnp.int32, (MOBA_BLOCK, cols), 1) & (rows - 1)
    own0 = pl.multiple_of(own * MOBA_BLOCK, MOBA_BLOCK)

    def values_t(n0, n, pair, head):
        lo = pair * LANES + head * HEAD_DIM
        vt = jnp.concatenate([vt_ref[n0 + u, lo:lo + HEAD_DIM, :] for u in range(n)], axis=1)
        return jnp.concatenate([vt, jnp.ones((ONES_ROWS, n * MOBA_BLOCK), BF16)], axis=0)

    def weighted_values(n0, n, pair, p):
        return [jnp.dot(values_t(n0, n, pair, head), p[:, head * rows:(head + 1) * rows],
                        preferred_element_type=F32) for head in range(2)]

    pairs = range(MOBA_PAIRS_PER_STEP)
    q_both, gates, own_scores = [], [], []
    for pair in pairs:
        lanes = slice(pair * LANES, (pair + 1) * LANES)
        q = q_ref[:, lanes]
        zero = jnp.zeros_like(q)
        q_both.append(jnp.concatenate([jnp.where(first_head, q, zero), jnp.where(first_head, zero, q)], axis=0))
        gates.append(lax.dot_general(km_ref[:, lanes].astype(BF16), q_both[pair], contract_last,
                                     preferred_element_type=F32))
        own_scores.append(lax.dot_general(k_ref[pl.ds(own0, MOBA_BLOCK), lanes], q_both[pair], contract_last,
                                          preferred_element_type=F32))
        s_ref[pair] = lax.dot_general(k_ref[pl.ds(0, group_keys), lanes], q_both[pair], contract_last,
                                      preferred_element_type=F32)

    q_augs, own_top, own_p, first_max = [], [], [], []
    for pair in pairs:
        gate = jnp.where(blk < own, gates[pair], -jnp.inf)
        chosen = jnp.zeros(gate.shape, jnp.bool_)
        for _ in range(MOBA_TOPK):
            best = jnp.max(gate, axis=0, keepdims=True)
            at = jnp.min(jnp.where(gate == best, blk, n_blocks), axis=0, keepdims=True)
            hit = blk == at
            chosen = chosen | (hit & (best > -jnp.inf))
            gate = jnp.where(hit, -jnp.inf, gate)
        bias_t = jnp.concatenate([jnp.where(chosen, 0.0, MASKED_SCORE),
                                  jnp.full((LANES - n_blocks, cols), MASKED_SCORE, F32)], axis=0)
        q_augs.append(jnp.concatenate([q_both[pair], bias_t.T.astype(BF16)], axis=1))
        first_bias = jnp.concatenate([jnp.broadcast_to(bias_t[u:u + 1], (MOBA_BLOCK, cols))
                                      for u in range(MOBA_GROUP)], axis=0)
        s = s_ref[pair] + first_bias
        s_ref[pair] = s
        first_max.append(jnp.max(s, axis=0, keepdims=True))

        s = jnp.where(key_pos <= query_pos, own_scores[pair], MASKED_SCORE)
        own_top.append(jnp.max(s, axis=0, keepdims=True))
        own_p.append(jnp.exp2(s - own_top[pair]).astype(BF16))

    init = []
    for pair in pairs:
        acc0 = weighted_values(own, 1, pair, own_p[pair])
        init.append((own_top[pair], acc0, first_max[pair]))

    def scores(n0, pair, s_ref):
        k0 = pl.multiple_of(n0 * MOBA_BLOCK, group_keys)
        k_aug = jnp.concatenate([k_ref[pl.ds(k0, group_keys), pair * LANES:(pair + 1) * LANES],
                                 tag_ref[pl.ds(k0, group_keys), :]], axis=1)
        s = lax.dot_general(k_aug, q_augs[pair], contract_last, preferred_element_type=F32)
        s_ref[pair] = s
        return jnp.max(s, axis=0, keepdims=True)

    def past_group(j, carry):
        n0 = MOBA_GROUP * j
        n_next = jnp.minimum(n0 + MOBA_GROUP, n_blocks - MOBA_GROUP)
        out = []
        for pair, (m, acc, s_max) in enumerate(carry):
            m_new = jnp.maximum(m, s_max)
            p = jnp.exp2(s_ref[pair] - m_new).astype(BF16)
            alpha = jnp.exp2(m - m_new)
            s_max = scores(n_next, pair, s_ref)
            pv = weighted_values(n0, MOBA_GROUP, pair, p)
            acc = [alpha[:, head * rows:(head + 1) * rows] * acc[head] + pv[head] for head in range(2)]
            out.append((m_new, acc, s_max))
        return out

    n_groups = (own + MOBA_GROUP - 1) // MOBA_GROUP
    final = lax.fori_loop(0, n_groups, past_group, init)
    for pair, (_, acc, _) in enumerate(final):
        out_t = [acc[head][:HEAD_DIM] * (1.0 / acc[head][HEAD_DIM:HEAD_DIM + 1]) for head in range(2)]
        o_ref[:, pair * LANES:(pair + 1) * LANES] = jnp.concatenate(out_t, axis=0).T.astype(BF16)


def _moba(q, k, vt, key_means, batch, seq):
    n_blocks = seq // MOBA_BLOCK
    q3 = q.reshape(batch, seq, ATTN_DIM)
    k3 = k.reshape(batch, seq, ATTN_DIM)
    km3 = key_means.reshape(batch, n_blocks, ATTN_DIM)
    tag = (jnp.arange(seq, dtype=jnp.int32)[:, None] // MOBA_BLOCK
           == jnp.arange(LANES, dtype=jnp.int32)[None, :]).astype(BF16)
    width = MOBA_PAIRS_PER_STEP * LANES
    o = pl.pallas_call(
        _moba_body,
        grid=(batch, ATTN_DIM // width, n_blocks),
        in_specs=[pl.BlockSpec((None, MOBA_BLOCK, width), lambda b, hp, i: (b, i, hp)),
                  pl.BlockSpec((None, seq, width), lambda b, hp, i: (b, 0, hp)),
                  pl.BlockSpec((n_blocks, width, MOBA_BLOCK), lambda b, hp, i: (b, hp, 0)),
                  _resident((seq, LANES)),
                  pl.BlockSpec((None, n_blocks, width), lambda b, hp, i: (b, 0, hp))],
        out_specs=pl.BlockSpec((None, MOBA_BLOCK, width), lambda b, hp, i: (b, i, hp)),
        out_shape=jax.ShapeDtypeStruct((batch, seq, ATTN_DIM), BF16),
        scratch_shapes=[pltpu.VMEM((MOBA_PAIRS_PER_STEP, MOBA_GROUP * MOBA_BLOCK, 2 * MOBA_BLOCK), F32)],
        compiler_params=pltpu.CompilerParams(
            dimension_semantics=("parallel", "parallel", "arbitrary"), vmem_limit_bytes=VMEM_LIMIT),
        name="moba",
    )(q3, k3, vt, tag, km3)
    return o.reshape(batch * seq, ATTN_DIM)


def _wo_b_body(h_ref, o_ref, w_ref, out_ref):
    out_ref[...] = h_ref[...] + jnp.dot(o_ref[...], w_ref[...], preferred_element_type=F32)


def _wo_b(h, o, w, *, tm=512):
    m = h.shape[0]
    row = pl.BlockSpec((tm, D_MODEL), lambda i: (i, 0))
    return pl.pallas_call(
        _wo_b_body,
        grid=(m // tm,),
        in_specs=[row, row, _resident((ATTN_DIM, D_MODEL))],
        out_specs=row,
        out_shape=jax.ShapeDtypeStruct((m, D_MODEL), F32),
        compiler_params=pltpu.CompilerParams(
            dimension_semantics=("parallel",), vmem_limit_bytes=VMEM_LIMIT),
        name="wo_b",
    )(h, o, w)


def kernel(x, ffn1_norm, ffn1_w_gate, ffn1_w_up, ffn1_w_down, mix_norm, ffn2_norm, ffn2_w_gate,
           ffn2_w_up, ffn2_w_down, a_w_qkv, a_w_o, kv_norm, kv_w, b_w_q, b_w_o, final_norm):
    batch, seq, _ = x.shape
    bf = lambda w: w.astype(BF16)
    rot_q = _rotary_tables(seq, QUERY_SCALE)
    rot_k = _rotary_tables(seq, 1.0)
    h = x.reshape(batch * seq, D_MODEL)

    h = _ffn(h, ffn1_norm[0], bf(ffn1_w_gate[0]), bf(ffn1_w_up[0]), bf(ffn1_w_down[0]))
    perm = _class_perm(TOKEN_TILE)
    per = TOKEN_TILE // A_CLASSES
    order = jnp.arange(seq, dtype=jnp.int32).reshape(seq // TOKEN_TILE, per, A_CLASSES)
    order = order.transpose(0, 2, 1).reshape(seq)
    qkv = _proj(h, mix_norm[0], bf(a_w_qkv[0]), rot_q[:, order], rot_k[:, order],
                groups_a=N_HEAD_PAIRS, groups_b=N_HEAD_PAIRS, class_perm=perm)[0]
    h = _wo_a(h, _attn_a(qkv, batch, seq), perm.T, bf(a_w_o[0]))
    h = _ffn(h, ffn2_norm[0], bf(ffn2_w_gate[0]), bf(ffn2_w_up[0]), bf(ffn2_w_down[0]))

    k, key_means, vt = _proj(h, kv_norm, bf(kv_w[:, :ATTN_DIM]), rot_k, rot_k, groups_a=0,
                             groups_b=N_HEAD_PAIRS, values_t_w=bf(kv_w[:, ATTN_DIM:].T))

    h = _ffn(h, ffn1_norm[1], bf(ffn1_w_gate[1]), bf(ffn1_w_up[1]), bf(ffn1_w_down[1]))
    q = _proj(h, mix_norm[1], bf(b_w_q[0]), rot_q, rot_q, groups_a=N_HEAD_PAIRS, groups_b=0)[0]
    h = _wo_b(h, _moba(q, k, vt, key_means, batch, seq), bf(b_w_o[0]))
    h = _ffn(h, ffn2_norm[1], bf(ffn2_w_gate[1]), bf(ffn2_w_up[1]), bf(ffn2_w_down[1]),
             final_g=final_norm)
    return h.reshape(batch, seq, D_MODEL)
```

```python
import functools

import jax
import jax.numpy as jnp
from jax import lax
from jax.experimental import pallas as pl
from jax.experimental.pallas import tpu as pltpu

D_MODEL = 1024
N_HEADS = 16
HEAD_DIM = 64
ATTN_DIM = N_HEADS * HEAD_DIM
ROT_DIM = HEAD_DIM // 4
ROPE_THETA = 500000.0
D_FF = 2816
FFN_RESIDUAL_WEIGHT = 0.5
DILATION_PAIRS = ((128, 1), (512, 4), (2048, 16))
MOBA_BLOCK = 256
MOBA_TOPK = 3
RMS_EPS = 1e-6

LANES = 128
N_HEAD_PAIRS = ATTN_DIM // LANES
BAND_BLOCK = 128
A_CLASSES = 16
TOKEN_TILE = 512
MASKED_SCORE = -1e30
QUERY_SCALE = HEAD_DIM ** -0.5 * 1.4426950408889634
VMEM_LIMIT = 56 * 1024 * 1024

F32 = jnp.float32
BF16 = jnp.bfloat16


def _rms(x, g):
    return x * lax.rsqrt(jnp.mean(x * x, axis=-1, keepdims=True) + RMS_EPS) * g


def _resident(shape):
    zeros = (0,) * len(shape)
    return pl.BlockSpec(shape, lambda *_: zeros, pipeline_mode=pl.Buffered(1))


def _rotate(t, rot_ref):
    return (t * rot_ref[0]
            + pltpu.roll(t, ROT_DIM // 2, 1) * rot_ref[1]
            + pltpu.roll(t, LANES - ROT_DIM // 2, 1) * rot_ref[2])


def _ffn_body(*refs, mixer, unpermute, query_proj, final_norm):
    refs = list(refs)
    x_ref, g_ref, wg_ref, wu_ref, wd_ref = refs[:5]
    del refs[:5]
    x = x_ref[...]
    if mixer:
        attn_ref, wo_ref = refs[:2]
        del refs[:2]
        attn = attn_ref[...].reshape(x.shape[0], ATTN_DIM)
        if unpermute:
            attn = jnp.dot(refs.pop(0)[...], attn, preferred_element_type=F32).astype(BF16)
        x = x + jnp.dot(attn, wo_ref[...], preferred_element_type=F32)
    hn = _rms(x, g_ref[...]).astype(BF16)
    gate = jnp.dot(hn, wg_ref[...], preferred_element_type=F32)
    up = jnp.dot(hn, wu_ref[...], preferred_element_type=F32)
    act = (jax.nn.silu(gate) * up).astype(BF16)
    y = x + FFN_RESIDUAL_WEIGHT * jnp.dot(act, wd_ref[...], preferred_element_type=F32)
    if query_proj:
        g2_ref, wq_ref, rot_ref = refs[:3]
        del refs[:3]
    if final_norm:
        y = _rms(y, refs.pop(0)[...])
    refs[0][...] = y
    if query_proj:
        q_ref = refs[1]
        q = jnp.dot(_rms(y, g2_ref[...]).astype(BF16), wq_ref[...], preferred_element_type=F32)
        for j in range(q.shape[1] // LANES):
            lanes = slice(j * LANES, (j + 1) * LANES)
            q_ref[:, lanes] = _rotate(q[:, lanes], rot_ref).astype(BF16)


def _ffn(h, g, wg, wu, wd, *, mixer=None, query_proj=None, final_g=None, tm=TOKEN_TILE):
    m = h.shape[0]
    row = pl.BlockSpec((tm, D_MODEL), lambda i: (i, 0))
    in_specs = [row, _resident((1, D_MODEL)), _resident((D_MODEL, D_FF)),
                _resident((D_MODEL, D_FF)), _resident((D_FF, D_MODEL))]
    args = [h, g.reshape(1, D_MODEL), wg, wu, wd]
    unpermute = False
    if mixer is not None:
        attn, w_o, unperm = mixer
        unpermute = unperm is not None
        if unpermute:
            tiles_per_seq = attn.shape[2] * A_CLASSES // tm
            in_specs.append(pl.BlockSpec((None, A_CLASSES, tm // A_CLASSES, ATTN_DIM),
                                         lambda i: (i // tiles_per_seq, 0, i % tiles_per_seq, 0)))
        else:
            in_specs.append(pl.BlockSpec((tm, ATTN_DIM), lambda i: (i, 0)))
        in_specs.append(_resident((ATTN_DIM, D_MODEL)))
        args += [attn, w_o]
        if unpermute:
            in_specs.append(_resident((tm, tm)))
            args.append(unperm)
    out_shape = [jax.ShapeDtypeStruct((m, D_MODEL), F32)]
    out_specs = [row]
    if query_proj is not None:
        g2, wq, rot = query_proj
        tiles_per_seq = rot.shape[1] // tm
        in_specs += [_resident((1, D_MODEL)), _resident(wq.shape),
                     pl.BlockSpec((3, tm, LANES), lambda i: (0, i % tiles_per_seq, 0))]
        args += [g2.reshape(1, D_MODEL), wq, rot]
        out_shape.append(jax.ShapeDtypeStruct((m, wq.shape[1]), BF16))
        out_specs.append(pl.BlockSpec((tm, wq.shape[1]), lambda i: (i, 0)))
    if final_g is not None:
        in_specs.append(_resident((1, D_MODEL)))
        args.append(final_g.reshape(1, D_MODEL))
    return pl.pallas_call(
        functools.partial(_ffn_body, mixer=mixer is not None, unpermute=unpermute,
                          query_proj=query_proj is not None, final_norm=final_g is not None),
        grid=(m // tm,),
        in_specs=in_specs,
        out_specs=out_specs,
        out_shape=out_shape,
        compiler_params=pltpu.CompilerParams(
            dimension_semantics=("parallel",), vmem_limit_bytes=VMEM_LIMIT),
        name="ffn",
    )(*args)


def _rotary_tables(seq, scale):
    half = ROT_DIM // 2
    inv_freq = ROPE_THETA ** (-jnp.arange(half, dtype=F32) / half)
    ang = jnp.arange(seq, dtype=F32)[:, None] * inv_freq[None, :]
    cos, sin = jnp.cos(ang), jnp.sin(ang)
    zeros = jnp.zeros((seq, HEAD_DIM - ROT_DIM), F32)
    z8 = jnp.zeros((seq, half), F32)
    c = jnp.concatenate([cos, cos, jnp.ones_like(zeros)], axis=1)
    s_up = jnp.concatenate([z8, sin, zeros], axis=1)
    s_dn = jnp.concatenate([-sin, z8, zeros], axis=1)
    tab = jnp.stack([c, s_up, s_dn], axis=0) * scale
    return jnp.tile(tab, (1, 1, LANES // HEAD_DIM))


def _proj_body(*refs, groups_a, groups_b, shared_kv, class_major):
    if shared_kv:
        x_ref, g_ref, w_ref, rot_a_ref, rot_b_ref, wvt_ref, o_ref, km_ref, vt_ref = refs
    elif class_major:
        x_ref, g_ref, w_ref, rot_a_ref, rot_b_ref, perm_ref, o_ref = refs
    else:
        x_ref, g_ref, w_ref, rot_a_ref, rot_b_ref, o_ref = refs
    x = x_ref[...]
    hn = _rms(x, g_ref[...]).astype(BF16)
    if class_major:
        hn = jnp.dot(perm_ref[...], hn, preferred_element_type=F32).astype(BF16)
    y = jnp.dot(hn, w_ref[...], preferred_element_type=F32)
    tm, n = y.shape
    for j in range(n // LANES):
        t = y[:, j * LANES:(j + 1) * LANES]
        rot_ref = rot_a_ref if j < groups_a else (rot_b_ref if j < groups_a + groups_b else None)
        if rot_ref is not None:
            t = _rotate(t, rot_ref)
            if shared_kv:
                km_ref[:, 0, j * LANES:(j + 1) * LANES] = jnp.mean(
                    t.reshape(tm // MOBA_BLOCK, MOBA_BLOCK, LANES), axis=1)
        if class_major:
            o_ref[:, :, j * LANES:(j + 1) * LANES] = t.reshape(
                A_CLASSES, tm // A_CLASSES, LANES).astype(BF16)
        else:
            o_ref[:, j * LANES:(j + 1) * LANES] = t.astype(BF16)
    if shared_kv:
        vt = lax.dot_general(wvt_ref[...], hn, (((1,), (1,)), ((), ())), preferred_element_type=F32)
        for j in range(tm // MOBA_BLOCK):
            vt_ref[j] = vt[:, j * MOBA_BLOCK:(j + 1) * MOBA_BLOCK].astype(BF16)


def _proj(h, g, w, rot_a, rot_b, *, groups_a, groups_b, values_t_w=None, class_perm=None, tm=TOKEN_TILE):
    m = h.shape[0]
    n = w.shape[1]
    seq = rot_a.shape[1]
    tiles_per_seq = seq // tm
    shared_kv = values_t_w is not None
    class_major = class_perm is not None
    rot_spec = pl.BlockSpec((3, tm, LANES), lambda i: (0, i % tiles_per_seq, 0))
    in_specs = [pl.BlockSpec((tm, D_MODEL), lambda i: (i, 0)), _resident((1, D_MODEL)),
                _resident((D_MODEL, n)), rot_spec, rot_spec]
    args = [h, g.reshape(1, D_MODEL), w, rot_a, rot_b]
    out_shape = [jax.ShapeDtypeStruct((m, n), BF16)]
    out_specs = [pl.BlockSpec((tm, n), lambda i: (i, 0))]
    if shared_kv:
        nb = tm // MOBA_BLOCK
        in_specs.append(_resident((ATTN_DIM, D_MODEL)))
        args.append(values_t_w)
        out_shape += [jax.ShapeDtypeStruct((m // MOBA_BLOCK, 1, n), F32),
                      jax.ShapeDtypeStruct((m // MOBA_BLOCK, ATTN_DIM, MOBA_BLOCK), BF16)]
        out_specs += [pl.BlockSpec((nb, 1, n), lambda i: (i, 0, 0)),
                      pl.BlockSpec((nb, ATTN_DIM, MOBA_BLOCK), lambda i: (i, 0, 0))]
    if class_major:
        in_specs.append(_resident((tm, tm)))
        args.append(class_perm)
        out_shape = [jax.ShapeDtypeStruct((m // seq, A_CLASSES, seq // A_CLASSES, n), BF16)]
        out_specs = [pl.BlockSpec((None, A_CLASSES, tm // A_CLASSES, n),
                                  lambda i: (i // tiles_per_seq, 0, i % tiles_per_seq, 0))]
    return pl.pallas_call(
        functools.partial(_proj_body, groups_a=groups_a, groups_b=groups_b, shared_kv=shared_kv,
                          class_major=class_major),
        grid=(m // tm,),
        in_specs=in_specs,
        out_specs=out_specs,
        out_shape=out_shape,
        compiler_params=pltpu.CompilerParams(
            dimension_semantics=("parallel",), vmem_limit_bytes=VMEM_LIMIT),
        name="proj",
    )(*args)


def _first_head(rows):
    return lax.broadcasted_iota(jnp.int32, (rows, LANES), 1) < HEAD_DIM


def _band_scores(q, k):
    fq = _first_head(q.shape[0])
    zero = jnp.zeros_like(q)
    q_both = jnp.concatenate([jnp.where(fq, q, zero), jnp.where(fq, zero, q)], axis=0)
    return lax.dot_general(q_both, k, (((1,), (1,)), ((), ())), preferred_element_type=F32)


def _band_softmax(s, v, bias):
    nq, nk = bias.shape
    fq, fk = _first_head(nq), _first_head(nk)
    one = jnp.ones_like(v)
    outs, tops = [], []
    for head in range(2):
        sh = s[head * nq:(head + 1) * nq] + bias
        top = jnp.max(sh, axis=-1, keepdims=True)
        p = jnp.exp2(sh - top).astype(BF16)
        vh = jnp.where(fk, v, one) if head == 0 else jnp.where(fk, one, v)
        outs.append(jnp.dot(p, vh, preferred_element_type=F32))
        tops.append(top)
    acc = jnp.where(fq, outs[0], outs[1])
    den = pltpu.roll(jnp.where(fq, outs[1], outs[0]), HEAD_DIM, 1)
    return acc, jnp.where(fq, tops[0], tops[1]), den


def _merge_softmax(acc_a, top_a, den_a, acc_b, top_b, den_b):
    top = jnp.maximum(top_a, top_b)
    wa, wb = jnp.exp2(top_a - top), jnp.exp2(top_b - top)
    return wa * acc_a + wb * acc_b, top, wa * den_a + wb * den_b


def _attn_a_body(q_ref, k_ref, v_ref, b16_ref, b4_ref, b1_ref, o_ref, acc_ref, top_ref, den_ref):
    n_idx = q_ref.shape[1]

    def class_tiles(r0, carry):
        classes = [4 * r0 + u for u in range(4)]
        scores = [_band_scores(q_ref[r], k_ref[r]) for r in classes]
        for r, s in zip(classes, scores):
            acc_ref[r], top_ref[r], den_ref[r] = _band_softmax(s, v_ref[r], b16_ref[...])
        return carry

    lax.fori_loop(0, A_CLASSES // 4, class_tiles, 0)

    def tiles_d4(it_lo, carry):
        tiles = [(it, r4) for it in (it_lo, it_lo + n_idx // 64) for r4 in range(4)]

        def window(it, r4):
            i0 = pl.multiple_of(it * 32, 32)
            j0 = pl.multiple_of(jnp.maximum(it - 1, 0) * 32, 32)
            return i0, j0, [r4 + 4 * c for c in range(4)]

        scores = []
        for it, r4 in tiles:
            i0, j0, classes = window(it, r4)
            q = jnp.concatenate([q_ref[r, pl.ds(i0, 32), :] for r in classes], axis=0)
            k = jnp.concatenate([k_ref[r, pl.ds(j0, 64), :] for r in classes], axis=0)
            scores.append(_band_scores(q, k))
        for (it, r4), s in zip(tiles, scores):
            i0, j0, classes = window(it, r4)
            v = jnp.concatenate([v_ref[r, pl.ds(j0, 64), :] for r in classes], axis=0)
            acc, top, den = _band_softmax(s, v, b4_ref[jnp.minimum(it, 1)])
            for c, r in enumerate(classes):
                at = (r, pl.ds(i0, 32), slice(None))
                part = slice(c * 32, (c + 1) * 32)
                acc_ref[at], top_ref[at], den_ref[at] = _merge_softmax(
                    acc_ref[at], top_ref[at], den_ref[at], acc[part], top[part], den[part])
        return carry

    lax.fori_loop(0, n_idx // 64, tiles_d4, 0)

    def scores_d1(it):
        i0 = pl.multiple_of(it * 16, 16)
        j0 = pl.multiple_of(jnp.maximum(it - 1, 0) * 16, 16)
        q = jnp.concatenate([q_ref[r, pl.ds(i0, 16), :] for r in range(A_CLASSES)], axis=0)
        k = jnp.concatenate([k_ref[r, pl.ds(j0, 32), :] for r in range(A_CLASSES)], axis=0)
        return _band_scores(q, k)

    def finish_d1(it, first, s):
        i0 = pl.multiple_of(it * 16, 16)
        j0 = pl.multiple_of(jnp.maximum(it - 1, 0) * 16, 16)
        v = jnp.concatenate([v_ref[r, pl.ds(j0, 32), :] for r in range(A_CLASSES)], axis=0)
        acc, top, den = _band_softmax(s, v, b1_ref[first])
        for r in range(A_CLASSES):
            at = (r, pl.ds(i0, 16), slice(None))
            part = slice(r * 16, (r + 1) * 16)
            acc_m, _, den_m = _merge_softmax(
                acc_ref[at], top_ref[at], den_ref[at], acc[part], top[part], den[part])
            o_ref[at] = (acc_m / den_m).astype(BF16)

    half = n_idx // 32

    def tiles_d1(it, carry):
        s_lo, s_hi = scores_d1(it), scores_d1(it + half)
        finish_d1(it, jnp.minimum(it, 1), s_lo)
        finish_d1(it + half, 1, s_hi)
        return carry

    lax.fori_loop(0, half, tiles_d1, 0)


def _band_bias(n_q_classes, q_len, k_len, dilation_step, offset):
    cq = jnp.arange(n_q_classes * q_len, dtype=jnp.int32) // q_len
    iq = jnp.arange(n_q_classes * q_len, dtype=jnp.int32) % q_len
    ck = jnp.arange(n_q_classes * k_len, dtype=jnp.int32) // k_len
    ik = jnp.arange(n_q_classes * k_len, dtype=jnp.int32) % k_len
    dist = dilation_step * (offset + iq[:, None] - ik[None, :]) + (cq[:, None] - ck[None, :])
    return jnp.where((dist >= 0) & (dist <= BAND_BLOCK), 0.0, MASKED_SCORE).astype(F32)


def _attn_a(qkv, batch, seq):
    n_idx = seq // A_CLASSES
    b16 = _band_bias(1, n_idx, n_idx, 1, 0)
    b4 = jnp.stack([_band_bias(4, 32, 64, 4, off) for off in (0, 32)])
    b1 = jnp.stack([_band_bias(A_CLASSES, 16, 32, A_CLASSES, off) for off in (0, 16)])

    def spec(which):
        return pl.BlockSpec((None, A_CLASSES, n_idx, LANES),
                            lambda b, hp: (b, 0, 0, which * N_HEAD_PAIRS + hp))

    return pl.pallas_call(
        _attn_a_body,
        grid=(batch, N_HEAD_PAIRS),
        in_specs=[spec(0), spec(1), spec(2),
                  _resident(b16.shape), _resident(b4.shape), _resident(b1.shape)],
        out_specs=pl.BlockSpec((None, A_CLASSES, n_idx, LANES), lambda b, hp: (b, 0, 0, hp)),
        out_shape=jax.ShapeDtypeStruct((batch, A_CLASSES, n_idx, ATTN_DIM), BF16),
        scratch_shapes=[pltpu.VMEM((A_CLASSES, n_idx, LANES), F32)] * 3,
        compiler_params=pltpu.CompilerParams(
            dimension_semantics=("parallel", "parallel"), vmem_limit_bytes=VMEM_LIMIT),
        name="attn_a",
    )(qkv, qkv, qkv, b16, b4, b1)


def _class_perm(tm):
    per = tm // A_CLASSES
    dst = jnp.arange(tm, dtype=jnp.int32)
    src = A_CLASSES * (dst % per) + dst // per
    return (src[:, None] == jnp.arange(tm, dtype=jnp.int32)[None, :]).astype(BF16)


MOBA_GROUP = 4
ONES_ROWS = 16
MOBA_PAIRS_PER_STEP = 4


def _moba_body(q_ref, k_ref, vt_ref, tag_ref, km_ref, o_ref, s_ref):
    own = pl.program_id(2)
    rows = q_ref.shape[0]
    n_blocks = km_ref.shape[0]
    cols = 2 * rows
    group_keys = MOBA_GROUP * MOBA_BLOCK
    contract_last = (((1,), (1,)), ((), ()))
    first_head = _first_head(rows)
    blk = lax.broadcasted_iota(jnp.int32, (n_blocks, cols), 0)
    key_pos = lax.broadcasted_iota(jnp.int32, (MOBA_BLOCK, cols), 0)
    query_pos = lax.broadcasted_iota(jnp.int32, (MOBA_BLOCK, cols), 1) & (rows - 1)
    own0 = pl.multiple_of(own * MOBA_BLOCK, MOBA_BLOCK)

    def values_t(n0, n, pair, head):
        lo = pair * LANES + head * HEAD_DIM
        vt = jnp.concatenate([vt_ref[n0 + u, lo:lo + HEAD_DIM, :] for u in range(n)], axis=1)
        return jnp.concatenate([vt, jnp.ones((ONES_ROWS, n * MOBA_BLOCK), BF16)], axis=0)

    def weighted_values(n0, n, pair, p):
        return [jnp.dot(values_t(n0, n, pair, head), p[:, head * rows:(head + 1) * rows],
                        preferred_element_type=F32) for head in range(2)]

    pairs = range(MOBA_PAIRS_PER_STEP)
    q_both, gates, own_scores = [], [], []
    for pair in pairs:
        lanes = slice(pair * LANES, (pair + 1) * LANES)
        q = q_ref[:, lanes]
        zero = jnp.zeros_like(q)
        q_both.append(jnp.concatenate([jnp.where(first_head, q, zero), jnp.where(first_head, zero, q)], axis=0))
        gates.append(lax.dot_general(km_ref[:, lanes].astype(BF16), q_both[pair], contract_last,
                                     preferred_element_type=F32))
        own_scores.append(lax.dot_general(k_ref[pl.ds(own0, MOBA_BLOCK), lanes], q_both[pair], contract_last,
                                          preferred_element_type=F32))
        s_ref[pair] = lax.dot_general(k_ref[pl.ds(0, group_keys), lanes], q_both[pair], contract_last,
                                      preferred_element_type=F32)

    q_augs, own_top, own_p, first_max = [], [], [], []
    for pair in pairs:
        gate = jnp.where(blk < own, gates[pair], -jnp.inf)
        chosen = jnp.zeros(gate.shape, jnp.bool_)
        for _ in range(MOBA_TOPK):
            best = jnp.max(gate, axis=0, keepdims=True)
            at = jnp.min(jnp.where(gate == best, blk, n_blocks), axis=0, keepdims=True)
            hit = blk == at
            chosen = chosen | (hit & (best > -jnp.inf))
            gate = jnp.where(hit, -jnp.inf, gate)
        bias_t = jnp.concatenate([jnp.where(chosen, 0.0, MASKED_SCORE),
                                  jnp.full((LANES - n_blocks, cols), MASKED_SCORE, F32)], axis=0)
        q_augs.append(jnp.concatenate([q_both[pair], bias_t.T.astype(BF16)], axis=1))
        first_bias = jnp.concatenate([jnp.broadcast_to(bias_t[u:u + 1], (MOBA_BLOCK, cols))
                                      for u in range(MOBA_GROUP)], axis=0)
        s = s_ref[pair] + first_bias
        s_ref[pair] = s
        first_max.append(jnp.max(s, axis=0, keepdims=True))

        s = jnp.where(key_pos <= query_pos, own_scores[pair], MASKED_SCORE)
        own_top.append(jnp.max(s, axis=0, keepdims=True))
        own_p.append(jnp.exp2(s - own_top[pair]).astype(BF16))

    init = []
    for pair in pairs:
        acc0 = weighted_values(own, 1, pair, own_p[pair])
        init.append((own_top[pair], acc0, first_max[pair]))

    def scores(n0, pair, s_ref):
        k0 = pl.multiple_of(n0 * MOBA_BLOCK, group_keys)
        k_aug = jnp.concatenate([k_ref[pl.ds(k0, group_keys), pair * LANES:(pair + 1) * LANES],
                                 tag_ref[pl.ds(k0, group_keys), :]], axis=1)
        s = lax.dot_general(k_aug, q_augs[pair], contract_last, preferred_element_type=F32)
        s_ref[pair] = s
        return jnp.max(s, axis=0, keepdims=True)

    def past_group(j, carry):
        n0 = MOBA_GROUP * j
        n_next = jnp.minimum(n0 + MOBA_GROUP, n_blocks - MOBA_GROUP)
        out = []
        for pair, (m, acc, s_max) in enumerate(carry):
            m_new = jnp.maximum(m, s_max)
            p = jnp.exp2(s_ref[pair] - m_new).astype(BF16)
            alpha = jnp.exp2(m - m_new)
            s_max = scores(n_next, pair, s_ref)
            pv = weighted_values(n0, MOBA_GROUP, pair, p)
            acc = [alpha[:, head * rows:(head + 1) * rows] * acc[head] + pv[head] for head in range(2)]
            out.append((m_new, acc, s_max))
        return out

    n_groups = (own + MOBA_GROUP - 1) // MOBA_GROUP
    final = lax.fori_loop(0, n_groups, past_group, init)
    for pair, (_, acc, _) in enumerate(final):
        out_t = [acc[head][:HEAD_DIM] * (1.0 / acc[head][HEAD_DIM:HEAD_DIM + 1]) for head in range(2)]
        o_ref[:, pair * LANES:(pair + 1) * LANES] = jnp.concatenate(out_t, axis=0).T.astype(BF16)


def _moba(q, k, vt, key_means, batch, seq):
    n_blocks = seq // MOBA_BLOCK
    q3 = q.reshape(batch, seq, ATTN_DIM)
    k3 = k.reshape(batch, seq, ATTN_DIM)
    km3 = key_means.reshape(batch, n_blocks, ATTN_DIM)
    tag = (jnp.arange(seq, dtype=jnp.int32)[:, None] // MOBA_BLOCK
           == jnp.arange(LANES, dtype=jnp.int32)[None, :]).astype(BF16)
    width = MOBA_PAIRS_PER_STEP * LANES
    o = pl.pallas_call(
        _moba_body,
        grid=(batch, ATTN_DIM // width, n_blocks),
        in_specs=[pl.BlockSpec((None, MOBA_BLOCK, width), lambda b, hp, i: (b, i, hp)),
                  pl.BlockSpec((None, seq, width), lambda b, hp, i: (b, 0, hp)),
                  pl.BlockSpec((n_blocks, width, MOBA_BLOCK), lambda b, hp, i: (b, hp, 0)),
                  _resident((seq, LANES)),
                  pl.BlockSpec((None, n_blocks, width), lambda b, hp, i: (b, 0, hp))],
        out_specs=pl.BlockSpec((None, MOBA_BLOCK, width), lambda b, hp, i: (b, i, hp)),
        out_shape=jax.ShapeDtypeStruct((batch, seq, ATTN_DIM), BF16),
        scratch_shapes=[pltpu.VMEM((MOBA_PAIRS_PER_STEP, MOBA_GROUP * MOBA_BLOCK, 2 * MOBA_BLOCK), F32)],
        compiler_params=pltpu.CompilerParams(
            dimension_semantics=("parallel", "parallel", "arbitrary"), vmem_limit_bytes=VMEM_LIMIT),
        name="moba",
    )(q3, k3, vt, tag, km3)
    return o.reshape(batch * seq, ATTN_DIM)


def kernel(x, ffn1_norm, ffn1_w_gate, ffn1_w_up, ffn1_w_down, mix_norm, ffn2_norm, ffn2_w_gate,
           ffn2_w_up, ffn2_w_down, a_w_qkv, a_w_o, kv_norm, kv_w, b_w_q, b_w_o, final_norm):
    batch, seq, _ = x.shape
    bf = lambda w: w.astype(BF16)
    rot_q = _rotary_tables(seq, QUERY_SCALE)
    rot_k = _rotary_tables(seq, 1.0)
    h = x.reshape(batch * seq, D_MODEL)

    h = _ffn(h, ffn1_norm[0], bf(ffn1_w_gate[0]), bf(ffn1_w_up[0]), bf(ffn1_w_down[0]))[0]
    perm = _class_perm(TOKEN_TILE)
    per = TOKEN_TILE // A_CLASSES
    order = jnp.arange(seq, dtype=jnp.int32).reshape(seq // TOKEN_TILE, per, A_CLASSES)
    order = order.transpose(0, 2, 1).reshape(seq)
    qkv = _proj(h, mix_norm[0], bf(a_w_qkv[0]), rot_q[:, order], rot_k[:, order],
                groups_a=N_HEAD_PAIRS, groups_b=N_HEAD_PAIRS, class_perm=perm)[0]
    h = _ffn(h, ffn2_norm[0], bf(ffn2_w_gate[0]), bf(ffn2_w_up[0]), bf(ffn2_w_down[0]),
             mixer=(_attn_a(qkv, batch, seq), bf(a_w_o[0]), perm.T))[0]

    k, key_means, vt = _proj(h, kv_norm, bf(kv_w[:, :ATTN_DIM]), rot_k, rot_k, groups_a=0,
                             groups_b=N_HEAD_PAIRS, values_t_w=bf(kv_w[:, ATTN_DIM:].T))

    h, q = _ffn(h, ffn1_norm[1], bf(ffn1_w_gate[1]), bf(ffn1_w_up[1]), bf(ffn1_w_down[1]),
                query_proj=(mix_norm[1], bf(b_w_q[0]), rot_q))
    h = _ffn(h, ffn2_norm[1], bf(ffn2_w_gate[1]), bf(ffn2_w_up[1]), bf(ffn2_w_down[1]),
             mixer=(_moba(q, k, vt, key_means, batch, seq), bf(b_w_o[0]), None), final_g=final_norm)[0]
    return h.reshape(batch, seq, D_MODEL)
```

```python
import functools

import jax
import jax.numpy as jnp
import numpy as np
from jax import lax
from jax.experimental import pallas as pl
from jax.experimental.pallas import tpu as pltpu

D_MODEL = 1024
N_HEADS = 16
HEAD_DIM = 64
ATTN_DIM = N_HEADS * HEAD_DIM
ROT_DIM = HEAD_DIM // 4
ROPE_THETA = 500000.0
D_FF = 2816
FFN_RESIDUAL_WEIGHT = 0.5
DILATION_PAIRS = ((128, 1), (512, 4), (2048, 16))
MOBA_BLOCK = 256
MOBA_TOPK = 3
RMS_EPS = 1e-6

LANES = 128
N_HEAD_PAIRS = ATTN_DIM // LANES
BAND_BLOCK = 128
A_CLASSES = 16
TOKEN_TILE = 512
MASKED_SCORE = -1e30
QUERY_SCALE = HEAD_DIM ** -0.5 * 1.4426950408889634
VMEM_LIMIT = 56 * 1024 * 1024

F32 = jnp.float32
BF16 = jnp.bfloat16


def _rms(x, g):
    return x * lax.rsqrt(jnp.mean(x * x, axis=-1, keepdims=True) + RMS_EPS) * g


def _resident(shape):
    zeros = (0,) * len(shape)
    return pl.BlockSpec(shape, lambda *_: zeros, pipeline_mode=pl.Buffered(1))


def _rotate(t, rot_ref):
    return (t * rot_ref[0]
            + pltpu.roll(t, ROT_DIM // 2, 1) * rot_ref[1]
            + pltpu.roll(t, LANES - ROT_DIM // 2, 1) * rot_ref[2])


def _ffn_body(*refs, mixer, unpermute, query_proj, final_norm):
    refs = list(refs)
    x_ref, g_ref, wg_ref, wu_ref, wd_ref = refs[:5]
    del refs[:5]
    x = x_ref[...]
    if mixer:
        attn_ref, wo_ref = refs[:2]
        del refs[:2]
        attn = attn_ref[...].reshape(x.shape[0], ATTN_DIM)
        if unpermute:
            attn = jnp.dot(refs.pop(0)[...], attn, preferred_element_type=F32).astype(BF16)
        x = x + jnp.dot(attn, wo_ref[...], preferred_element_type=F32)
    hn = _rms(x, g_ref[...]).astype(BF16)
    gate = jnp.dot(hn, wg_ref[...], preferred_element_type=F32)
    up = jnp.dot(hn, wu_ref[...], preferred_element_type=F32)
    act = (jax.nn.silu(gate) * up).astype(BF16)
    y = x + FFN_RESIDUAL_WEIGHT * jnp.dot(act, wd_ref[...], preferred_element_type=F32)
    if query_proj:
        g2_ref, wq_ref, rot_ref = refs[:3]
        del refs[:3]
    if final_norm:
        y = _rms(y, refs.pop(0)[...])
    refs[0][...] = y
    if query_proj:
        q_ref = refs[1]
        q = jnp.dot(_rms(y, g2_ref[...]).astype(BF16), wq_ref[...], preferred_element_type=F32)
        for j in range(q.shape[1] // LANES):
            lanes = slice(j * LANES, (j + 1) * LANES)
            q_ref[:, lanes] = _rotate(q[:, lanes], rot_ref).astype(BF16)


def _ffn(h, g, wg, wu, wd, *, mixer=None, query_proj=None, final_g=None, tm=TOKEN_TILE):
    m = h.shape[0]
    row = pl.BlockSpec((tm, D_MODEL), lambda i: (i, 0))
    in_specs = [row, _resident((1, D_MODEL)), _resident((D_MODEL, D_FF)),
                _resident((D_MODEL, D_FF)), _resident((D_FF, D_MODEL))]
    args = [h, g.reshape(1, D_MODEL), wg, wu, wd]
    unpermute = False
    if mixer is not None:
        attn, w_o, unperm = mixer
        unpermute = unperm is not None
        if unpermute:
            tiles_per_seq = attn.shape[2] * A_CLASSES // tm
            in_specs.append(pl.BlockSpec((None, A_CLASSES, tm // A_CLASSES, ATTN_DIM),
                                         lambda i: (i // tiles_per_seq, 0, i % tiles_per_seq, 0)))
        else:
            in_specs.append(pl.BlockSpec((tm, ATTN_DIM), lambda i: (i, 0)))
        in_specs.append(_resident((ATTN_DIM, D_MODEL)))
        args += [attn, w_o]
        if unpermute:
            in_specs.append(_resident((tm, tm)))
            args.append(unperm)
    out_shape = [jax.ShapeDtypeStruct((m, D_MODEL), F32)]
    out_specs = [row]
    if query_proj is not None:
        g2, wq, rot = query_proj
        tiles_per_seq = rot.shape[1] // tm
        in_specs += [_resident((1, D_MODEL)), _resident(wq.shape),
                     pl.BlockSpec((3, tm, LANES), lambda i: (0, i % tiles_per_seq, 0))]
        args += [g2.reshape(1, D_MODEL), wq, rot]
        out_shape.append(jax.ShapeDtypeStruct((m, wq.shape[1]), BF16))
        out_specs.append(pl.BlockSpec((tm, wq.shape[1]), lambda i: (i, 0)))
    if final_g is not None:
        in_specs.append(_resident((1, D_MODEL)))
        args.append(final_g.reshape(1, D_MODEL))
    return pl.pallas_call(
        functools.partial(_ffn_body, mixer=mixer is not None, unpermute=unpermute,
                          query_proj=query_proj is not None, final_norm=final_g is not None),
        grid=(m // tm,),
        in_specs=in_specs,
        out_specs=out_specs,
        out_shape=out_shape,
        compiler_params=pltpu.CompilerParams(
            dimension_semantics=("parallel",), vmem_limit_bytes=VMEM_LIMIT),
        name="ffn",
    )(*args)


def _rotary_tables(positions, scale):
    half = ROT_DIM // 2
    n = len(positions)
    inv_freq = ROPE_THETA ** (-np.arange(half, dtype=np.float64) / half)
    ang = positions.astype(np.float64)[:, None] * inv_freq[None, :]
    cos, sin = np.cos(ang), np.sin(ang)
    zeros = np.zeros((n, HEAD_DIM - ROT_DIM))
    z8 = np.zeros((n, half))
    c = np.concatenate([cos, cos, np.ones_like(zeros)], axis=1)
    s_up = np.concatenate([z8, sin, zeros], axis=1)
    s_dn = np.concatenate([-sin, z8, zeros], axis=1)
    tab = np.stack([c, s_up, s_dn], axis=0) * scale
    return jnp.asarray(np.tile(tab, (1, 1, LANES // HEAD_DIM)), F32)


def _proj_body(*refs, groups_a, groups_b, shared_kv, class_major):
    if shared_kv:
        x_ref, g_ref, w_ref, rot_a_ref, rot_b_ref, wvt_ref, o_ref, km_ref, vt_ref = refs
    elif class_major:
        x_ref, g_ref, w_ref, rot_a_ref, rot_b_ref, perm_ref, o_ref = refs
    else:
        x_ref, g_ref, w_ref, rot_a_ref, rot_b_ref, o_ref = refs
    x = x_ref[...]
    hn = _rms(x, g_ref[...]).astype(BF16)
    if class_major:
        hn = jnp.dot(perm_ref[...], hn, preferred_element_type=F32).astype(BF16)
    y = jnp.dot(hn, w_ref[...], preferred_element_type=F32)
    tm, n = y.shape
    for j in range(n // LANES):
        t = y[:, j * LANES:(j + 1) * LANES]
        rot_ref = rot_a_ref if j < groups_a else (rot_b_ref if j < groups_a + groups_b else None)
        if rot_ref is not None:
            t = _rotate(t, rot_ref)
            if shared_kv:
                km_ref[:, 0, j * LANES:(j + 1) * LANES] = jnp.mean(
                    t.reshape(tm // MOBA_BLOCK, MOBA_BLOCK, LANES), axis=1)
        if class_major:
            o_ref[:, :, j * LANES:(j + 1) * LANES] = t.reshape(
                A_CLASSES, tm // A_CLASSES, LANES).astype(BF16)
        else:
            o_ref[:, j * LANES:(j + 1) * LANES] = t.astype(BF16)
    if shared_kv:
        vt = lax.dot_general(wvt_ref[...], hn, (((1,), (1,)), ((), ())), preferred_element_type=F32)
        for j in range(tm // MOBA_BLOCK):
            vt_ref[j] = vt[:, j * MOBA_BLOCK:(j + 1) * MOBA_BLOCK].astype(BF16)


def _proj(h, g, w, rot_a, rot_b, *, groups_a, groups_b, values_t_w=None, class_perm=None, tm=TOKEN_TILE):
    m = h.shape[0]
    n = w.shape[1]
    seq = rot_a.shape[1]
    tiles_per_seq = seq // tm
    shared_kv = values_t_w is not None
    class_major = class_perm is not None
    rot_spec = pl.BlockSpec((3, tm, LANES), lambda i: (0, i % tiles_per_seq, 0))
    in_specs = [pl.BlockSpec((tm, D_MODEL), lambda i: (i, 0)), _resident((1, D_MODEL)),
                _resident((D_MODEL, n)), rot_spec, rot_spec]
    args = [h, g.reshape(1, D_MODEL), w, rot_a, rot_b]
    out_shape = [jax.ShapeDtypeStruct((m, n), BF16)]
    out_specs = [pl.BlockSpec((tm, n), lambda i: (i, 0))]
    if shared_kv:
        nb = tm // MOBA_BLOCK
        in_specs.append(_resident((ATTN_DIM, D_MODEL)))
        args.append(values_t_w)
        out_shape += [jax.ShapeDtypeStruct((m // MOBA_BLOCK, 1, n), F32),
                      jax.ShapeDtypeStruct((m // MOBA_BLOCK, ATTN_DIM, MOBA_BLOCK), BF16)]
        out_specs += [pl.BlockSpec((nb, 1, n), lambda i: (i, 0, 0)),
                      pl.BlockSpec((nb, ATTN_DIM, MOBA_BLOCK), lambda i: (i, 0, 0))]
    if class_major:
        in_specs.append(_resident((tm, tm)))
        args.append(class_perm)
        out_shape = [jax.ShapeDtypeStruct((m // seq, A_CLASSES, seq // A_CLASSES, n), BF16)]
        out_specs = [pl.BlockSpec((None, A_CLASSES, tm // A_CLASSES, n),
                                  lambda i: (i // tiles_per_seq, 0, i % tiles_per_seq, 0))]
    return pl.pallas_call(
        functools.partial(_proj_body, groups_a=groups_a, groups_b=groups_b, shared_kv=shared_kv,
                          class_major=class_major),
        grid=(m // tm,),
        in_specs=in_specs,
        out_specs=out_specs,
        out_shape=out_shape,
        compiler_params=pltpu.CompilerParams(
            dimension_semantics=("parallel",), vmem_limit_bytes=VMEM_LIMIT),
        name="proj",
    )(*args)


def _first_head(rows):
    return lax.broadcasted_iota(jnp.int32, (rows, LANES), 1) < HEAD_DIM


def _band_scores(q, k):
    fq = _first_head(q.shape[0])
    zero = jnp.zeros_like(q)
    q_both = jnp.concatenate([jnp.where(fq, q, zero), jnp.where(fq, zero, q)], axis=0)
    return lax.dot_general(q_both, k, (((1,), (1,)), ((), ())), preferred_element_type=F32)


def _band_softmax(s, v, bias):
    nq, nk = bias.shape
    fq, fk = _first_head(nq), _first_head(nk)
    one = jnp.ones_like(v)
    outs, tops = [], []
    for head in range(2):
        sh = s[head * nq:(head + 1) * nq] + bias
        top = jnp.max(sh, axis=-1, keepdims=True)
        p = jnp.exp2(sh - top).astype(BF16)
        vh = jnp.where(fk, v, one) if head == 0 else jnp.where(fk, one, v)
        outs.append(jnp.dot(p, vh, preferred_element_type=F32))
        tops.append(top)
    acc = jnp.where(fq, outs[0], outs[1])
    den = pltpu.roll(jnp.where(fq, outs[1], outs[0]), HEAD_DIM, 1)
    return acc, jnp.where(fq, tops[0], tops[1]), den


def _merge_softmax(acc_a, top_a, den_a, acc_b, top_b, den_b):
    top = jnp.maximum(top_a, top_b)
    wa, wb = jnp.exp2(top_a - top), jnp.exp2(top_b - top)
    return wa * acc_a + wb * acc_b, top, wa * den_a + wb * den_b


def _attn_a_body(q_ref, k_ref, v_ref, b16_ref, b4_ref, b1_ref, o_ref, acc_ref, top_ref, den_ref):
    n_idx = q_ref.shape[1]

    def class_tiles(r0, carry):
        classes = [8 * r0 + u for u in range(8)]
        scores = [_band_scores(q_ref[r], k_ref[r]) for r in classes]
        for r, s in zip(classes, scores):
            acc_ref[r], top_ref[r], den_ref[r] = _band_softmax(s, v_ref[r], b16_ref[...])
        return carry

    lax.fori_loop(0, A_CLASSES // 8, class_tiles, 0)

    def tiles_d4(it_lo, carry):
        tiles = [(it, r4) for it in (it_lo, it_lo + n_idx // 64) for r4 in range(4)]

        def window(it, r4):
            i0 = pl.multiple_of(it * 32, 32)
            j0 = pl.multiple_of(jnp.maximum(it - 1, 0) * 32, 32)
            return i0, j0, [r4 + 4 * c for c in range(4)]

        scores = []
        for it, r4 in tiles:
            i0, j0, classes = window(it, r4)
            q = jnp.concatenate([q_ref[r, pl.ds(i0, 32), :] for r in classes], axis=0)
            k = jnp.concatenate([k_ref[r, pl.ds(j0, 64), :] for r in classes], axis=0)
            scores.append(_band_scores(q, k))
        for (it, r4), s in zip(tiles, scores):
            i0, j0, classes = window(it, r4)
            v = jnp.concatenate([v_ref[r, pl.ds(j0, 64), :] for r in classes], axis=0)
            acc, top, den = _band_softmax(s, v, b4_ref[jnp.minimum(it, 1)])
            for c, r in enumerate(classes):
                at = (r, pl.ds(i0, 32), slice(None))
                part = slice(c * 32, (c + 1) * 32)
                acc_ref[at], top_ref[at], den_ref[at] = _merge_softmax(
                    acc_ref[at], top_ref[at], den_ref[at], acc[part], top[part], den[part])
        return carry

    lax.fori_loop(0, n_idx // 64, tiles_d4, 0)

    def scores_d1(it):
        i0 = pl.multiple_of(it * 16, 16)
        j0 = pl.multiple_of(jnp.maximum(it - 1, 0) * 16, 16)
        q = jnp.concatenate([q_ref[r, pl.ds(i0, 16), :] for r in range(A_CLASSES)], axis=0)
        k = jnp.concatenate([k_ref[r, pl.ds(j0, 32), :] for r in range(A_CLASSES)], axis=0)
        return _band_scores(q, k)

    def finish_d1(it, first, s):
        i0 = pl.multiple_of(it * 16, 16)
        j0 = pl.multiple_of(jnp.maximum(it - 1, 0) * 16, 16)
        v = jnp.concatenate([v_ref[r, pl.ds(j0, 32), :] for r in range(A_CLASSES)], axis=0)
        acc, top, den = _band_softmax(s, v, b1_ref[first])
        for r in range(A_CLASSES):
            at = (r, pl.ds(i0, 16), slice(None))
            part = slice(r * 16, (r + 1) * 16)
            acc_m, _, den_m = _merge_softmax(
                acc_ref[at], top_ref[at], den_ref[at], acc[part], top[part], den[part])
            o_ref[at] = (acc_m / den_m).astype(BF16)

    quarter = n_idx // 64

    def tiles_d1(it, carry):
        tiles = [it + u * quarter for u in range(4)]
        scores = [scores_d1(t) for t in tiles]
        finish_d1(tiles[0], jnp.minimum(it, 1), scores[0])
        for t, s in zip(tiles[1:], scores[1:]):
            finish_d1(t, 1, s)
        return carry

    lax.fori_loop(0, quarter, tiles_d1, 0)


def _band_bias(n_q_classes, q_len, k_len, dilation_step, offset):
    cq = np.arange(n_q_classes * q_len) // q_len
    iq = np.arange(n_q_classes * q_len) % q_len
    ck = np.arange(n_q_classes * k_len) // k_len
    ik = np.arange(n_q_classes * k_len) % k_len
    dist = dilation_step * (offset + iq[:, None] - ik[None, :]) + (cq[:, None] - ck[None, :])
    return np.where((dist >= 0) & (dist <= BAND_BLOCK), 0.0, MASKED_SCORE).astype(np.float32)


def _attn_a(qkv, batch, seq):
    n_idx = seq // A_CLASSES
    b16 = jnp.asarray(_band_bias(1, n_idx, n_idx, 1, 0))
    b4 = jnp.asarray(np.stack([_band_bias(4, 32, 64, 4, off) for off in (0, 32)]))
    b1 = jnp.asarray(np.stack([_band_bias(A_CLASSES, 16, 32, A_CLASSES, off) for off in (0, 16)]))

    def spec(which):
        return pl.BlockSpec((None, A_CLASSES, n_idx, LANES),
                            lambda b, hp: (b, 0, 0, which * N_HEAD_PAIRS + hp))

    return pl.pallas_call(
        _attn_a_body,
        grid=(batch, N_HEAD_PAIRS),
        in_specs=[spec(0), spec(1), spec(2),
                  _resident(b16.shape), _resident(b4.shape), _resident(b1.shape)],
        out_specs=pl.BlockSpec((None, A_CLASSES, n_idx, LANES), lambda b, hp: (b, 0, 0, hp)),
        out_shape=jax.ShapeDtypeStruct((batch, A_CLASSES, n_idx, ATTN_DIM), BF16),
        scratch_shapes=[pltpu.VMEM((A_CLASSES, n_idx, LANES), F32)] * 3,
        compiler_params=pltpu.CompilerParams(
            dimension_semantics=("parallel", "parallel"), vmem_limit_bytes=VMEM_LIMIT),
        name="attn_a",
    )(qkv, qkv, qkv, b16, b4, b1)


def _class_perm(tm):
    per = tm // A_CLASSES
    dst = np.arange(tm)
    src = A_CLASSES * (dst % per) + dst // per
    return (src[:, None] == np.arange(tm)[None, :]).astype(np.float32)


MOBA_GROUP = 4
ONES_ROWS = 16
MOBA_PAIRS_PER_STEP = 4


def _moba_body(q_ref, k_ref, vt_ref, tag_ref, km_ref, o_ref, s_ref):
    own = pl.program_id(2)
    rows = q_ref.shape[0]
    n_blocks = km_ref.shape[0]
    cols = 2 * rows
    group_keys = MOBA_GROUP * MOBA_BLOCK
    contract_last = (((1,), (1,)), ((), ()))
    first_head = _first_head(rows)
    blk = lax.broadcasted_iota(jnp.int32, (n_blocks, cols), 0)
    key_pos = lax.broadcasted_iota(jnp.int32, (MOBA_BLOCK, cols), 0)
    query_pos = lax.broadcasted_iota(jnp.int32, (MOBA_BLOCK, cols), 1) & (rows - 1)
    own0 = pl.multiple_of(own * MOBA_BLOCK, MOBA_BLOCK)

    def values_t(n0, n, pair, head):
        lo = pair * LANES + head * HEAD_DIM
        vt = jnp.concatenate([vt_ref[n0 + u, lo:lo + HEAD_DIM, :] for u in range(n)], axis=1)
        return jnp.concatenate([vt, jnp.ones((ONES_ROWS, n * MOBA_BLOCK), BF16)], axis=0)

    def weighted_values(n0, n, pair, p):
        return [jnp.dot(values_t(n0, n, pair, head), p[:, head * rows:(head + 1) * rows],
                        preferred_element_type=F32) for head in range(2)]

    pairs = range(MOBA_PAIRS_PER_STEP)
    q_both, gates, own_scores = [], [], []
    for pair in pairs:
        lanes = slice(pair * LANES, (pair + 1) * LANES)
        q = q_ref[:, lanes]
        zero = jnp.zeros_like(q)
        q_both.append(jnp.concatenate([jnp.where(first_head, q, zero), jnp.where(first_head, zero, q)], axis=0))
        gates.append(lax.dot_general(km_ref[:, lanes].astype(BF16), q_both[pair], contract_last,
                                     preferred_element_type=F32))
        own_scores.append(lax.dot_general(k_ref[pl.ds(own0, MOBA_BLOCK), lanes], q_both[pair], contract_last,
                                          preferred_element_type=F32))
        s_ref[pair] = lax.dot_general(k_ref[pl.ds(0, group_keys), lanes], q_both[pair], contract_last,
                                      preferred_element_type=F32)

    q_augs, own_top, own_p, first_max = [], [], [], []
    for pair in pairs:
        gate = jnp.where(blk < own, gates[pair], -jnp.inf)
        chosen = jnp.zeros(gate.shape, jnp.bool_)
        for _ in range(MOBA_TOPK):
            best = jnp.max(gate, axis=0, keepdims=True)
            at = jnp.min(jnp.where(gate == best, blk, n_blocks), axis=0, keepdims=True)
            hit = blk == at
            chosen = chosen | (hit & (best > -jnp.inf))
            gate = jnp.where(hit, -jnp.inf, gate)
        bias_t = jnp.concatenate([jnp.where(chosen, 0.0, MASKED_SCORE),
                                  jnp.full((LANES - n_blocks, cols), MASKED_SCORE, F32)], axis=0)
        q_augs.append(jnp.concatenate([q_both[pair], bias_t.T.astype(BF16)], axis=1))
        first_bias = jnp.concatenate([jnp.broadcast_to(bias_t[u:u + 1], (MOBA_BLOCK, cols))
                                      for u in range(MOBA_GROUP)], axis=0)
        s = s_ref[pair] + first_bias
        s_ref[pair] = s
        first_max.append(jnp.max(s, axis=0, keepdims=True))

        s = jnp.where(key_pos <= query_pos, own_scores[pair], MASKED_SCORE)
        own_top.append(jnp.max(s, axis=0, keepdims=True))
        own_p.append(jnp.exp2(s - own_top[pair]).astype(BF16))

    init = []
    for pair in pairs:
        acc0 = weighted_values(own, 1, pair, own_p[pair])
        init.append((own_top[pair], acc0, first_max[pair]))

    def scores(n0, pair, s_ref):
        k0 = pl.multiple_of(n0 * MOBA_BLOCK, group_keys)
        k_aug = jnp.concatenate([k_ref[pl.ds(k0, group_keys), pair * LANES:(pair + 1) * LANES],
                                 tag_ref[pl.ds(k0, group_keys), :]], axis=1)
        s = lax.dot_general(k_aug, q_augs[pair], contract_last, preferred_element_type=F32)
        s_ref[pair] = s
        return jnp.max(s, axis=0, keepdims=True)

    def absorb_group(j, carry, score_next):
        n0 = MOBA_GROUP * j
        out = []
        for pair, (m, acc, s_max) in enumerate(carry):
            m_new = jnp.maximum(m, s_max)
            p = jnp.exp2(s_ref[pair] - m_new).astype(BF16)
            alpha = jnp.exp2(m - m_new)
            if score_next:
                s_max = scores(n0 + MOBA_GROUP, pair, s_ref)
            pv = weighted_values(n0, MOBA_GROUP, pair, p)
            acc = [alpha[:, head * rows:(head + 1) * rows] * acc[head] + pv[head] for head in range(2)]
            out.append((m_new, acc, s_max))
        return out

    last_group = jnp.maximum((own + MOBA_GROUP - 1) // MOBA_GROUP - 1, 0)
    state = lax.fori_loop(0, last_group, functools.partial(absorb_group, score_next=True), init)
    final = absorb_group(last_group, state, score_next=False)
    for pair, (_, acc, _) in enumerate(final):
        out_t = [acc[head][:HEAD_DIM] * (1.0 / acc[head][HEAD_DIM:HEAD_DIM + 1]) for head in range(2)]
        o_ref[:, pair * LANES:(pair + 1) * LANES] = jnp.concatenate(out_t, axis=0).T.astype(BF16)


def _moba(q, k, vt, key_means, batch, seq):
    n_blocks = seq // MOBA_BLOCK
    q3 = q.reshape(batch, seq, ATTN_DIM)
    k3 = k.reshape(batch, seq, ATTN_DIM)
    km3 = key_means.reshape(batch, n_blocks, ATTN_DIM)
    tag = jnp.asarray(np.arange(seq)[:, None] // MOBA_BLOCK == np.arange(LANES)[None, :], BF16)
    width = MOBA_PAIRS_PER_STEP * LANES
    o = pl.pallas_call(
        _moba_body,
        grid=(batch, ATTN_DIM // width, n_blocks),
        in_specs=[pl.BlockSpec((None, MOBA_BLOCK, width), lambda b, hp, i: (b, i, hp)),
                  pl.BlockSpec((None, seq, width), lambda b, hp, i: (b, 0, hp)),
                  pl.BlockSpec((n_blocks, width, MOBA_BLOCK), lambda b, hp, i: (b, hp, 0)),
                  _resident((seq, LANES)),
                  pl.BlockSpec((None, n_blocks, width), lambda b, hp, i: (b, 0, hp))],
        out_specs=pl.BlockSpec((None, MOBA_BLOCK, width), lambda b, hp, i: (b, i, hp)),
        out_shape=jax.ShapeDtypeStruct((batch, seq, ATTN_DIM), BF16),
        scratch_shapes=[pltpu.VMEM((MOBA_PAIRS_PER_STEP, MOBA_GROUP * MOBA_BLOCK, 2 * MOBA_BLOCK), F32)],
        compiler_params=pltpu.CompilerParams(
            dimension_semantics=("parallel", "parallel", "arbitrary"), vmem_limit_bytes=VMEM_LIMIT),
        name="moba",
    )(q3, k3, vt, tag, km3)
    return o.reshape(batch * seq, ATTN_DIM)


def kernel(x, ffn1_norm, ffn1_w_gate, ffn1_w_up, ffn1_w_down, mix_norm, ffn2_norm, ffn2_w_gate,
           ffn2_w_up, ffn2_w_down, a_w_qkv, a_w_o, kv_norm, kv_w, b_w_q, b_w_o, final_norm):
    batch, seq, _ = x.shape
    bf = lambda w: w.astype(BF16)
    positions = np.arange(seq)
    per = TOKEN_TILE // A_CLASSES
    class_major = positions.reshape(seq // TOKEN_TILE, per, A_CLASSES).transpose(0, 2, 1).reshape(seq)
    perm = _class_perm(TOKEN_TILE)
    h = x.reshape(batch * seq, D_MODEL)

    h = _ffn(h, ffn1_norm[0], bf(ffn1_w_gate[0]), bf(ffn1_w_up[0]), bf(ffn1_w_down[0]))[0]
    qkv = _proj(h, mix_norm[0], bf(a_w_qkv[0]), _rotary_tables(class_major, QUERY_SCALE),
                _rotary_tables(class_major, 1.0), groups_a=N_HEAD_PAIRS, groups_b=N_HEAD_PAIRS,
                class_perm=jnp.asarray(perm, BF16))[0]
    h = _ffn(h, ffn2_norm[0], bf(ffn2_w_gate[0]), bf(ffn2_w_up[0]), bf(ffn2_w_down[0]),
             mixer=(_attn_a(qkv, batch, seq), bf(a_w_o[0]), jnp.asarray(perm.T, BF16)))[0]

    rot_k = _rotary_tables(positions, 1.0)
    k, key_means, vt = _proj(h, kv_norm, bf(kv_w[:, :ATTN_DIM]), rot_k, rot_k, groups_a=0,
                             groups_b=N_HEAD_PAIRS, values_t_w=bf(kv_w[:, ATTN_DIM:].T))

    h, q = _ffn(h, ffn1_norm[1], bf(ffn1_w_gate[1]), bf(ffn1_w_up[1]), bf(ffn1_w_down[1]),
                query_proj=(mix_norm[1], bf(b_w_q[0]), _rotary_tables(positions, QUERY_SCALE)))
    h = _ffn(h, ffn2_norm[1], bf(ffn2_w_gate[1]), bf(ffn2_w_up[1]), bf(ffn2_w_down[1]),
             mixer=(_moba(q, k, vt, key_means, batch, seq), bf(b_w_o[0]), None), final_g=final_norm)[0]
    return h.reshape(batch, seq, D_MODEL)
```

```python
import functools

import jax
import jax.numpy as jnp
import numpy as np
from jax import lax
from jax.experimental import pallas as pl
from jax.experimental.pallas import tpu as pltpu

D_MODEL = 1024
N_HEADS = 16
HEAD_DIM = 64
ATTN_DIM = N_HEADS * HEAD_DIM
ROT_DIM = HEAD_DIM // 4
ROPE_THETA = 500000.0
D_FF = 2816
FFN_RESIDUAL_WEIGHT = 0.5
DILATION_PAIRS = ((128, 1), (512, 4), (2048, 16))
MOBA_BLOCK = 256
MOBA_TOPK = 3
RMS_EPS = 1e-6

LANES = 128
N_HEAD_PAIRS = ATTN_DIM // LANES
BAND_BLOCK = 128
A_CLASSES = 16
TOKEN_TILE = 512
PERM_SLAB = 128
MASKED_SCORE = -1e30
QUERY_SCALE = HEAD_DIM ** -0.5 * 1.4426950408889634
VMEM_LIMIT = 56 * 1024 * 1024

F32 = jnp.float32
BF16 = jnp.bfloat16


def _rms(x, g):
    return x * lax.rsqrt(jnp.mean(x * x, axis=-1, keepdims=True) + RMS_EPS) * g


def _resident(shape):
    zeros = (0,) * len(shape)
    return pl.BlockSpec(shape, lambda *_: zeros, pipeline_mode=pl.Buffered(1))


def _rotate(t, rot_ref):
    return (t * rot_ref[0]
            + pltpu.roll(t, ROT_DIM // 2, 1) * rot_ref[1]
            + pltpu.roll(t, LANES - ROT_DIM // 2, 1) * rot_ref[2])


def _ffn_body(*refs, mixer, unpermute, query_proj, final_norm):
    refs = list(refs)
    x_ref, g_ref, wg_ref, wu_ref, wd_ref = refs[:5]
    del refs[:5]
    x = x_ref[...]
    if mixer:
        attn_ref, wo_ref = refs[:2]
        del refs[:2]
        if unpermute:
            unperm = refs.pop(0)[...]
            per = PERM_SLAB // A_CLASSES
            wide = attn_ref[...].astype(F32)
            slabs = []
            for s in range(x.shape[0] // PERM_SLAB):
                rows = jnp.concatenate([wide[r, s * per:(s + 1) * per] for r in range(A_CLASSES)], axis=0)
                slabs.append(jnp.dot(unperm, rows.astype(BF16), preferred_element_type=F32).astype(BF16))
            attn = jnp.concatenate(slabs, axis=0)
        else:
            attn = attn_ref[...]
        x = x + jnp.dot(attn, wo_ref[...], preferred_element_type=F32)
    hn = _rms(x, g_ref[...]).astype(BF16)
    gate = jnp.dot(hn, wg_ref[...], preferred_element_type=F32)
    up = jnp.dot(hn, wu_ref[...], preferred_element_type=F32)
    act = (jax.nn.silu(gate) * up).astype(BF16)
    y = x + FFN_RESIDUAL_WEIGHT * jnp.dot(act, wd_ref[...], preferred_element_type=F32)
    if query_proj:
        g2_ref, wq_ref, rot_ref = refs[:3]
        del refs[:3]
    if final_norm:
        y = _rms(y, refs.pop(0)[...])
    refs[0][...] = y
    if query_proj:
        q_ref = refs[1]
        q = jnp.dot(_rms(y, g2_ref[...]).astype(BF16), wq_ref[...], preferred_element_type=F32)
        for j in range(q.shape[1] // LANES):
            lanes = slice(j * LANES, (j + 1) * LANES)
            q_ref[:, lanes] = _rotate(q[:, lanes], rot_ref).astype(BF16)


def _ffn(h, g, wg, wu, wd, *, mixer=None, query_proj=None, final_g=None, tm=TOKEN_TILE):
    m = h.shape[0]
    row = pl.BlockSpec((tm, D_MODEL), lambda i: (i, 0))
    in_specs = [row, _resident((1, D_MODEL)), _resident((D_MODEL, D_FF)),
                _resident((D_MODEL, D_FF)), _resident((D_FF, D_MODEL))]
    args = [h, g.reshape(1, D_MODEL), wg, wu, wd]
    unpermute = False
    if mixer is not None:
        attn, w_o, unperm = mixer
        unpermute = unperm is not None
        if unpermute:
            tiles_per_seq = attn.shape[2] * A_CLASSES // tm
            in_specs.append(pl.BlockSpec((None, A_CLASSES, tm // A_CLASSES, ATTN_DIM),
                                         lambda i: (i // tiles_per_seq, 0, i % tiles_per_seq, 0)))
        else:
            in_specs.append(pl.BlockSpec((tm, ATTN_DIM), lambda i: (i, 0)))
        in_specs.append(_resident((ATTN_DIM, D_MODEL)))
        args += [attn, w_o]
        if unpermute:
            in_specs.append(_resident((PERM_SLAB, PERM_SLAB)))
            args.append(unperm)
    out_shape = [jax.ShapeDtypeStruct((m, D_MODEL), F32)]
    out_specs = [row]
    if query_proj is not None:
        g2, wq, rot = query_proj
        tiles_per_seq = rot.shape[1] // tm
        in_specs += [_resident((1, D_MODEL)), _resident(wq.shape),
                     pl.BlockSpec((3, tm, LANES), lambda i: (0, i % tiles_per_seq, 0))]
        args += [g2.reshape(1, D_MODEL), wq, rot]
        out_shape.append(jax.ShapeDtypeStruct((m, wq.shape[1]), BF16))
        out_specs.append(pl.BlockSpec((tm, wq.shape[1]), lambda i: (i, 0)))
    if final_g is not None:
        in_specs.append(_resident((1, D_MODEL)))
        args.append(final_g.reshape(1, D_MODEL))
    return pl.pallas_call(
        functools.partial(_ffn_body, mixer=mixer is not None, unpermute=unpermute,
                          query_proj=query_proj is not None, final_norm=final_g is not None),
        grid=(m // tm,),
        in_specs=in_specs,
        out_specs=out_specs,
        out_shape=out_shape,
        compiler_params=pltpu.CompilerParams(
            dimension_semantics=("parallel",), vmem_limit_bytes=VMEM_LIMIT),
        name="ffn",
    )(*args)


def _rotary_tables(positions, scale):
    half = ROT_DIM // 2
    n = len(positions)
    inv_freq = ROPE_THETA ** (-np.arange(half, dtype=np.float64) / half)
    ang = positions.astype(np.float64)[:, None] * inv_freq[None, :]
    cos, sin = np.cos(ang), np.sin(ang)
    zeros = np.zeros((n, HEAD_DIM - ROT_DIM))
    z8 = np.zeros((n, half))
    c = np.concatenate([cos, cos, np.ones_like(zeros)], axis=1)
    s_up = np.concatenate([z8, sin, zeros], axis=1)
    s_dn = np.concatenate([-sin, z8, zeros], axis=1)
    tab = np.stack([c, s_up, s_dn], axis=0) * scale
    return jnp.asarray(np.tile(tab, (1, 1, LANES // HEAD_DIM)), F32)


def _proj_body(*refs, groups_a, groups_b, shared_kv, class_major):
    if shared_kv:
        x_ref, g_ref, w_ref, rot_a_ref, rot_b_ref, wvt_ref, o_ref, km_ref, vt_ref = refs
    elif class_major:
        x_ref, g_ref, w_ref, rot_a_ref, rot_b_ref, perm_ref, o_ref = refs
    else:
        x_ref, g_ref, w_ref, rot_a_ref, rot_b_ref, o_ref = refs
    x = x_ref[...]
    hn = _rms(x, g_ref[...]).astype(BF16)
    if class_major:
        n_slabs = hn.shape[0] // PERM_SLAB
        hn = jnp.concatenate(
            [jnp.dot(perm_ref[...], hn[s * PERM_SLAB:(s + 1) * PERM_SLAB],
                     preferred_element_type=F32).astype(BF16) for s in range(n_slabs)], axis=0)
    y = jnp.dot(hn, w_ref[...], preferred_element_type=F32)
    tm, n = y.shape
    for j in range(n // LANES):
        t = y[:, j * LANES:(j + 1) * LANES]
        rot_ref = rot_a_ref if j < groups_a else (rot_b_ref if j < groups_a + groups_b else None)
        if rot_ref is not None:
            t = _rotate(t, rot_ref)
            if shared_kv:
                km_ref[:, 0, j * LANES:(j + 1) * LANES] = jnp.mean(
                    t.reshape(tm // MOBA_BLOCK, MOBA_BLOCK, LANES), axis=1)
        if class_major:
            per = PERM_SLAB // A_CLASSES
            for r in range(A_CLASSES):
                rows = [t[s * PERM_SLAB + r * per:s * PERM_SLAB + (r + 1) * per] for s in range(n_slabs)]
                o_ref[r, :, j * LANES:(j + 1) * LANES] = jnp.concatenate(rows, axis=0).astype(BF16)
        else:
            o_ref[:, j * LANES:(j + 1) * LANES] = t.astype(BF16)
    if shared_kv:
        vt = lax.dot_general(wvt_ref[...], hn, (((1,), (1,)), ((), ())), preferred_element_type=F32)
        for j in range(tm // MOBA_BLOCK):
            vt_ref[j] = vt[:, j * MOBA_BLOCK:(j + 1) * MOBA_BLOCK].astype(BF16)


def _proj(h, g, w, rot_a, rot_b, *, groups_a, groups_b, values_t_w=None, class_perm=None, tm=TOKEN_TILE):
    m = h.shape[0]
    n = w.shape[1]
    seq = rot_a.shape[1]
    tiles_per_seq = seq // tm
    shared_kv = values_t_w is not None
    class_major = class_perm is not None
    rot_spec = pl.BlockSpec((3, tm, LANES), lambda i: (0, i % tiles_per_seq, 0))
    in_specs = [pl.BlockSpec((tm, D_MODEL), lambda i: (i, 0)), _resident((1, D_MODEL)),
                _resident((D_MODEL, n)), rot_spec, rot_spec]
    args = [h, g.reshape(1, D_MODEL), w, rot_a, rot_b]
    out_shape = [jax.ShapeDtypeStruct((m, n), BF16)]
    out_specs = [pl.BlockSpec((tm, n), lambda i: (i, 0))]
    if shared_kv:
        nb = tm // MOBA_BLOCK
        in_specs.append(_resident((ATTN_DIM, D_MODEL)))
        args.append(values_t_w)
        out_shape += [jax.ShapeDtypeStruct((m // MOBA_BLOCK, 1, n), F32),
                      jax.ShapeDtypeStruct((m // MOBA_BLOCK, ATTN_DIM, MOBA_BLOCK), BF16)]
        out_specs += [pl.BlockSpec((nb, 1, n), lambda i: (i, 0, 0)),
                      pl.BlockSpec((nb, ATTN_DIM, MOBA_BLOCK), lambda i: (i, 0, 0))]
    if class_major:
        in_specs.append(_resident((PERM_SLAB, PERM_SLAB)))
        args.append(class_perm)
        out_shape = [jax.ShapeDtypeStruct((m // seq, A_CLASSES, seq // A_CLASSES, n), BF16)]
        out_specs = [pl.BlockSpec((None, A_CLASSES, tm // A_CLASSES, n),
                                  lambda i: (i // tiles_per_seq, 0, i % tiles_per_seq, 0))]
    return pl.pallas_call(
        functools.partial(_proj_body, groups_a=groups_a, groups_b=groups_b, shared_kv=shared_kv,
                          class_major=class_major),
        grid=(m // tm,),
        in_specs=in_specs,
        out_specs=out_specs,
        out_shape=out_shape,
        compiler_params=pltpu.CompilerParams(
            dimension_semantics=("parallel",), vmem_limit_bytes=VMEM_LIMIT),
        name="proj",
    )(*args)


def _first_head(rows):
    return lax.broadcasted_iota(jnp.int32, (rows, LANES), 1) < HEAD_DIM


def _band_scores(q, k):
    fq = _first_head(q.shape[0])
    zero = jnp.zeros_like(q)
    q_both = jnp.concatenate([jnp.where(fq, q, zero), jnp.where(fq, zero, q)], axis=0)
    return lax.dot_general(q_both, k, (((1,), (1,)), ((), ())), preferred_element_type=F32)


def _band_softmax(s, v, bias):
    nq, nk = bias.shape
    fq, fk = _first_head(nq), _first_head(nk)
    one = jnp.ones_like(v)
    outs, tops = [], []
    for head in range(2):
        sh = s[head * nq:(head + 1) * nq] + bias
        top = jnp.max(sh, axis=-1, keepdims=True)
        p = jnp.exp2(sh - top).astype(BF16)
        vh = jnp.where(fk, v, one) if head == 0 else jnp.where(fk, one, v)
        outs.append(jnp.dot(p, vh, preferred_element_type=F32))
        tops.append(top)
    acc = jnp.where(fq, outs[0], outs[1])
    den = pltpu.roll(jnp.where(fq, outs[1], outs[0]), HEAD_DIM, 1)
    return acc, jnp.where(fq, tops[0], tops[1]), den


def _merge_softmax(acc_a, top_a, den_a, acc_b, top_b, den_b):
    top = jnp.maximum(top_a, top_b)
    wa, wb = jnp.exp2(top_a - top), jnp.exp2(top_b - top)
    return wa * acc_a + wb * acc_b, top, wa * den_a + wb * den_b


def _attn_a_body(q_ref, k_ref, v_ref, b16_ref, b4_ref, b1_ref, o_ref, acc_ref, top_ref, den_ref):
    n_idx = q_ref.shape[1]

    def class_tiles(r0, carry):
        classes = [8 * r0 + u for u in range(8)]
        scores = [_band_scores(q_ref[r], k_ref[r]) for r in classes]
        for r, s in zip(classes, scores):
            acc_ref[r], top_ref[r], den_ref[r] = _band_softmax(s, v_ref[r], b16_ref[...])
        return carry

    lax.fori_loop(0, A_CLASSES // 8, class_tiles, 0)

    def tiles_d4(it_lo, carry):
        tiles = [(it, r4) for it in (it_lo, it_lo + n_idx // 64) for r4 in range(4)]

        def window(it, r4):
            i0 = pl.multiple_of(it * 32, 32)
            j0 = pl.multiple_of(jnp.maximum(it - 1, 0) * 32, 32)
            return i0, j0, [r4 + 4 * c for c in range(4)]

        scores = []
        for it, r4 in tiles:
            i0, j0, classes = window(it, r4)
            q = jnp.concatenate([q_ref[r, pl.ds(i0, 32), :] for r in classes], axis=0)
            k = jnp.concatenate([k_ref[r, pl.ds(j0, 64), :] for r in classes], axis=0)
            scores.append(_band_scores(q, k))
        for (it, r4), s in zip(tiles, scores):
            i0, j0, classes = window(it, r4)
            v = jnp.concatenate([v_ref[r, pl.ds(j0, 64), :] for r in classes], axis=0)
            acc, top, den = _band_softmax(s, v, b4_ref[jnp.minimum(it, 1)])
            for c, r in enumerate(classes):
                at = (r, pl.ds(i0, 32), slice(None))
                part = slice(c * 32, (c + 1) * 32)
                acc_ref[at], top_ref[at], den_ref[at] = _merge_softmax(
                    acc_ref[at], top_ref[at], den_ref[at], acc[part], top[part], den[part])
        return carry

    lax.fori_loop(0, n_idx // 64, tiles_d4, 0)

    def scores_d1(it):
        i0 = pl.multiple_of(it * 16, 16)
        j0 = pl.multiple_of(jnp.maximum(it - 1, 0) * 16, 16)
        q = jnp.concatenate([q_ref[r, pl.ds(i0, 16), :] for r in range(A_CLASSES)], axis=0)
        k = jnp.concatenate([k_ref[r, pl.ds(j0, 32), :] for r in range(A_CLASSES)], axis=0)
        return _band_scores(q, k)

    def finish_d1(it, first, s):
        i0 = pl.multiple_of(it * 16, 16)
        j0 = pl.multiple_of(jnp.maximum(it - 1, 0) * 16, 16)
        v = jnp.concatenate([v_ref[r, pl.ds(j0, 32), :] for r in range(A_CLASSES)], axis=0)
        acc, top, den = _band_softmax(s, v, b1_ref[first])
        for r in range(A_CLASSES):
            at = (r, pl.ds(i0, 16), slice(None))
            part = slice(r * 16, (r + 1) * 16)
            acc_m, _, den_m = _merge_softmax(
                acc_ref[at], top_ref[at], den_ref[at], acc[part], top[part], den[part])
            o_ref[at] = (acc_m / den_m).astype(BF16)

    quarter = n_idx // 64

    def tiles_d1(it, carry):
        tiles = [it + u * quarter for u in range(4)]
        scores = [scores_d1(t) for t in tiles]
        finish_d1(tiles[0], jnp.minimum(it, 1), scores[0])
        for t, s in zip(tiles[1:], scores[1:]):
            finish_d1(t, 1, s)
        return carry

    lax.fori_loop(0, quarter, tiles_d1, 0)


def _band_bias(n_q_classes, q_len, k_len, dilation_step, offset):
    cq = np.arange(n_q_classes * q_len) // q_len
    iq = np.arange(n_q_classes * q_len) % q_len
    ck = np.arange(n_q_classes * k_len) // k_len
    ik = np.arange(n_q_classes * k_len) % k_len
    dist = dilation_step * (offset + iq[:, None] - ik[None, :]) + (cq[:, None] - ck[None, :])
    return np.where((dist >= 0) & (dist <= BAND_BLOCK), 0.0, MASKED_SCORE).astype(np.float32)


def _attn_a(qkv, batch, seq):
    n_idx = seq // A_CLASSES
    b16 = jnp.asarray(_band_bias(1, n_idx, n_idx, 1, 0))
    b4 = jnp.asarray(np.stack([_band_bias(4, 32, 64, 4, off) for off in (0, 32)]))
    b1 = jnp.asarray(np.stack([_band_bias(A_CLASSES, 16, 32, A_CLASSES, off) for off in (0, 16)]))

    def spec(which):
        return pl.BlockSpec((None, A_CLASSES, n_idx, LANES),
                            lambda b, hp: (b, 0, 0, which * N_HEAD_PAIRS + hp))

    return pl.pallas_call(
        _attn_a_body,
        grid=(batch, N_HEAD_PAIRS),
        in_specs=[spec(0), spec(1), spec(2),
                  _resident(b16.shape), _resident(b4.shape), _resident(b1.shape)],
        out_specs=pl.BlockSpec((None, A_CLASSES, n_idx, LANES), lambda b, hp: (b, 0, 0, hp)),
        out_shape=jax.ShapeDtypeStruct((batch, A_CLASSES, n_idx, ATTN_DIM), BF16),
        scratch_shapes=[pltpu.VMEM((A_CLASSES, n_idx, LANES), F32)] * 3,
        compiler_params=pltpu.CompilerParams(
            dimension_semantics=("parallel", "parallel"), vmem_limit_bytes=VMEM_LIMIT),
        name="attn_a",
    )(qkv, qkv, qkv, b16, b4, b1)


def _class_perm(tm):
    per = tm // A_CLASSES
    dst = np.arange(tm)
    src = A_CLASSES * (dst % per) + dst // per
    return (src[:, None] == np.arange(tm)[None, :]).astype(np.float32)


MOBA_GROUP = 4
ONES_ROWS = 16
MOBA_PAIRS_PER_STEP = 4


def _moba_body(q_ref, k_ref, vt_ref, tag_ref, km_ref, o_ref, s_ref):
    own = pl.program_id(2)
    rows = q_ref.shape[0]
    n_blocks = km_ref.shape[0]
    cols = 2 * rows
    group_keys = MOBA_GROUP * MOBA_BLOCK
    contract_last = (((1,), (1,)), ((), ()))
    first_head = _first_head(rows)
    blk = lax.broadcasted_iota(jnp.int32, (n_blocks, cols), 0)
    key_pos = lax.broadcasted_iota(jnp.int32, (MOBA_BLOCK, cols), 0)
    query_pos = lax.broadcasted_iota(jnp.int32, (MOBA_BLOCK, cols), 1) & (rows - 1)
    own0 = pl.multiple_of(own * MOBA_BLOCK, MOBA_BLOCK)

    def values_t(n0, n, pair, head):
        lo = pair * LANES + head * HEAD_DIM
        vt = jnp.concatenate([vt_ref[n0 + u, lo:lo + HEAD_DIM, :] for u in range(n)], axis=1)
        return jnp.concatenate([vt, jnp.ones((ONES_ROWS, n * MOBA_BLOCK), BF16)], axis=0)

    def weighted_values(n0, n, pair, p):
        return [jnp.dot(values_t(n0, n, pair, head), p[:, head * rows:(head + 1) * rows],
                        preferred_element_type=F32) for head in range(2)]

    pairs = range(MOBA_PAIRS_PER_STEP)
    q_both, gates, own_scores = [], [], []
    for pair in pairs:
        lanes = slice(pair * LANES, (pair + 1) * LANES)
        q = q_ref[:, lanes]
        zero = jnp.zeros_like(q)
        q_both.append(jnp.concatenate([jnp.where(first_head, q, zero), jnp.where(first_head, zero, q)], axis=0))
        gates.append(lax.dot_general(km_ref[:, lanes].astype(BF16), q_both[pair], contract_last,
                                     preferred_element_type=F32))
        own_scores.append(lax.dot_general(k_ref[pl.ds(own0, MOBA_BLOCK), lanes], q_both[pair], contract_last,
                                          preferred_element_type=F32))
        s_ref[pair] = lax.dot_general(k_ref[pl.ds(0, group_keys), lanes], q_both[pair], contract_last,
                                      preferred_element_type=F32)

    q_augs, own_top, own_p, first_max = [], [], [], []
    for pair in pairs:
        gate = jnp.where(blk < own, gates[pair], -jnp.inf)
        chosen = jnp.zeros(gate.shape, jnp.bool_)
        for _ in range(MOBA_TOPK):
            best = jnp.max(gate, axis=0, keepdims=True)
            at = jnp.min(jnp.where(gate == best, blk, n_blocks), axis=0, keepdims=True)
            hit = blk == at
            chosen = chosen | (hit & (best > -jnp.inf))
            gate = jnp.where(hit, -jnp.inf, gate)
        bias_t = jnp.concatenate([jnp.where(chosen, 0.0, MASKED_SCORE),
                                  jnp.full((LANES - n_blocks, cols), MASKED_SCORE, F32)], axis=0)
        q_augs.append(jnp.concatenate([q_both[pair], bias_t.T.astype(BF16)], axis=1))
        first_bias = jnp.concatenate([jnp.broadcast_to(bias_t[u:u + 1], (MOBA_BLOCK, cols))
                                      for u in range(MOBA_GROUP)], axis=0)
        s = s_ref[pair] + first_bias
        s_ref[pair] = s
        first_max.append(jnp.max(s, axis=0, keepdims=True))

        s = jnp.where(key_pos <= query_pos, own_scores[pair], MASKED_SCORE)
        own_top.append(jnp.max(s, axis=0, keepdims=True))
        own_p.append(jnp.exp2(s - own_top[pair]).astype(BF16))

    init = []
    for pair in pairs:
        acc0 = weighted_values(own, 1, pair, own_p[pair])
        init.append((own_top[pair], acc0, first_max[pair]))

    def scores(n0, pair, s_ref):
        k0 = pl.multiple_of(n0 * MOBA_BLOCK, group_keys)
        k_aug = jnp.concatenate([k_ref[pl.ds(k0, group_keys), pair * LANES:(pair + 1) * LANES],
                                 tag_ref[pl.ds(k0, group_keys), :]], axis=1)
        s = lax.dot_general(k_aug, q_augs[pair], contract_last, preferred_element_type=F32)
        s_ref[pair] = s
        return jnp.max(s, axis=0, keepdims=True)

    def absorb_group(j, carry, score_next):
        n0 = MOBA_GROUP * j
        out = []
        for pair, (m, acc, s_max) in enumerate(carry):
            m_new = jnp.maximum(m, s_max)
            p = jnp.exp2(s_ref[pair] - m_new).astype(BF16)
            alpha = jnp.exp2(m - m_new)
            if score_next:
                s_max = scores(n0 + MOBA_GROUP, pair, s_ref)
            pv = weighted_values(n0, MOBA_GROUP, pair, p)
            acc = [alpha[:, head * rows:(head + 1) * rows] * acc[head] + pv[head] for head in range(2)]
            out.append((m_new, acc, s_max))
        return out

    last_group = jnp.maximum((own + MOBA_GROUP - 1) // MOBA_GROUP - 1, 0)
    state = lax.fori_loop(0, last_group, functools.partial(absorb_group, score_next=True), init)
    final = absorb_group(last_group, state, score_next=False)
    for pair, (_, acc, _) in enumerate(final):
        out_t = [acc[head][:HEAD_DIM] * (1.0 / acc[head][HEAD_DIM:HEAD_DIM + 1]) for head in range(2)]
        o_ref[:, pair * LANES:(pair + 1) * LANES] = jnp.concatenate(out_t, axis=0).T.astype(BF16)


def _moba(q, k, vt, key_means, batch, seq):
    n_blocks = seq // MOBA_BLOCK
    q3 = q.reshape(batch, seq, ATTN_DIM)
    k3 = k.reshape(batch, seq, ATTN_DIM)
    km3 = key_means.reshape(batch, n_blocks, ATTN_DIM)
    tag = jnp.asarray(np.arange(seq)[:, None] // MOBA_BLOCK == np.arange(LANES)[None, :], BF16)
    width = MOBA_PAIRS_PER_STEP * LANES
    o = pl.pallas_call(
        _moba_body,
        grid=(batch, ATTN_DIM // width, n_blocks),
        in_specs=[pl.BlockSpec((None, MOBA_BLOCK, width), lambda b, hp, i: (b, i, hp)),
                  pl.BlockSpec((None, seq, width), lambda b, hp, i: (b, 0, hp)),
                  pl.BlockSpec((n_blocks, width, MOBA_BLOCK), lambda b, hp, i: (b, hp, 0)),
                  _resident((seq, LANES)),
                  pl.BlockSpec((None, n_blocks, width), lambda b, hp, i: (b, 0, hp))],
        out_specs=pl.BlockSpec((None, MOBA_BLOCK, width), lambda b, hp, i: (b, i, hp)),
        out_shape=jax.ShapeDtypeStruct((batch, seq, ATTN_DIM), BF16),
        scratch_shapes=[pltpu.VMEM((MOBA_PAIRS_PER_STEP, MOBA_GROUP * MOBA_BLOCK, 2 * MOBA_BLOCK), F32)],
        compiler_params=pltpu.CompilerParams(
            dimension_semantics=("parallel", "parallel", "arbitrary"), vmem_limit_bytes=VMEM_LIMIT),
        name="moba",
    )(q3, k3, vt, tag, km3)
    return o.reshape(batch * seq, ATTN_DIM)


def kernel(x, ffn1_norm, ffn1_w_gate, ffn1_w_up, ffn1_w_down, mix_norm, ffn2_norm, ffn2_w_gate,
           ffn2_w_up, ffn2_w_down, a_w_qkv, a_w_o, kv_norm, kv_w, b_w_q, b_w_o, final_norm):
    batch, seq, _ = x.shape
    bf = lambda w: w.astype(BF16)
    positions = np.arange(seq)
    per = PERM_SLAB // A_CLASSES
    class_major = positions.reshape(seq // PERM_SLAB, per, A_CLASSES).transpose(0, 2, 1).reshape(seq)
    perm = _class_perm(PERM_SLAB)
    h = x.reshape(batch * seq, D_MODEL)

    h = _ffn(h, ffn1_norm[0], bf(ffn1_w_gate[0]), bf(ffn1_w_up[0]), bf(ffn1_w_down[0]))[0]
    qkv = _proj(h, mix_norm[0], bf(a_w_qkv[0]), _rotary_tables(class_major, QUERY_SCALE),
                _rotary_tables(class_major, 1.0), groups_a=N_HEAD_PAIRS, groups_b=N_HEAD_PAIRS,
                class_perm=jnp.asarray(perm, BF16))[0]
    h = _ffn(h, ffn2_norm[0], bf(ffn2_w_gate[0]), bf(ffn2_w_up[0]), bf(ffn2_w_down[0]),
             mixer=(_attn_a(qkv, batch, seq), bf(a_w_o[0]), jnp.asarray(perm.T, BF16)))[0]

    rot_k = _rotary_tables(positions, 1.0)
    k, key_means, vt = _proj(h, kv_norm, bf(kv_w[:, :ATTN_DIM]), rot_k, rot_k, groups_a=0,
                             groups_b=N_HEAD_PAIRS, values_t_w=bf(kv_w[:, ATTN_DIM:].T))

    h, q = _ffn(h, ffn1_norm[1], bf(ffn1_w_gate[1]), bf(ffn1_w_up[1]), bf(ffn1_w_down[1]),
                query_proj=(mix_norm[1], bf(b_w_q[0]), _rotary_tables(positions, QUERY_SCALE)))
    h = _ffn(h, ffn2_norm[1], bf(ffn2_w_gate[1]), bf(ffn2_w_up[1]), bf(ffn2_w_down[1]),
             mixer=(_moba(q, k, vt, key_means, batch, seq), bf(b_w_o[0]), None), final_g=final_norm)[0]
    return h.reshape(batch, seq, D_MODEL)
```

```python
import functools

import jax
import jax.numpy as jnp
import numpy as np
from jax import lax
from jax.experimental import pallas as pl
from jax.experimental.pallas import tpu as pltpu

D_MODEL = 1024
N_HEADS = 16
HEAD_DIM = 64
ATTN_DIM = N_HEADS * HEAD_DIM
ROT_DIM = HEAD_DIM // 4
ROPE_THETA = 500000.0
D_FF = 2816
FFN_RESIDUAL_WEIGHT = 0.5
DILATION_PAIRS = ((128, 1), (512, 4), (2048, 16))
MOBA_BLOCK = 256
MOBA_TOPK = 3
RMS_EPS = 1e-6

LANES = 128
N_HEAD_PAIRS = ATTN_DIM // LANES
BAND_BLOCK = 128
A_CLASSES = 16
TOKEN_TILE = 512
PERM_SLAB = 128
MASKED_SCORE = -1e30
QUERY_SCALE = HEAD_DIM ** -0.5 * 1.4426950408889634
VMEM_LIMIT = 56 * 1024 * 1024

F32 = jnp.float32
BF16 = jnp.bfloat16


def _rms(x, g):
    return x * lax.rsqrt(jnp.mean(x * x, axis=-1, keepdims=True) + RMS_EPS) * g


def _resident(shape):
    zeros = (0,) * len(shape)
    return pl.BlockSpec(shape, lambda *_: zeros, pipeline_mode=pl.Buffered(1))


def _rotate(t, rot_ref):
    return (t * rot_ref[0]
            + pltpu.roll(t, ROT_DIM // 2, 1) * rot_ref[1]
            + pltpu.roll(t, LANES - ROT_DIM // 2, 1) * rot_ref[2])


def _ffn_body(*refs, mixer, unpermute, query_proj, final_norm):
    refs = list(refs)
    x_ref, g_ref, wg_ref, wu_ref, wd_ref = refs[:5]
    del refs[:5]
    x = x_ref[...]
    if mixer:
        attn_ref, wo_ref = refs[:2]
        del refs[:2]
        if unpermute:
            unperm = refs.pop(0)[...]
            per = PERM_SLAB // A_CLASSES
            wide = attn_ref[...].astype(F32)
            slabs = []
            for s in range(x.shape[0] // PERM_SLAB):
                rows = jnp.concatenate([wide[r, s * per:(s + 1) * per] for r in range(A_CLASSES)], axis=0)
                slabs.append(jnp.dot(unperm, rows.astype(BF16), preferred_element_type=F32).astype(BF16))
            attn = jnp.concatenate(slabs, axis=0)
        else:
            attn = attn_ref[...]
        x = x + jnp.dot(attn, wo_ref[...], preferred_element_type=F32)
    hn = _rms(x, g_ref[...]).astype(BF16)
    gate = jnp.dot(hn, wg_ref[...], preferred_element_type=F32)
    up = jnp.dot(hn, wu_ref[...], preferred_element_type=F32)
    act = (jax.nn.silu(gate) * up).astype(BF16)
    y = x + FFN_RESIDUAL_WEIGHT * jnp.dot(act, wd_ref[...], preferred_element_type=F32)
    if query_proj:
        g2_ref, wq_ref, rot_ref = refs[:3]
        del refs[:3]
    if final_norm:
        y = _rms(y, refs.pop(0)[...])
    refs[0][...] = y
    if query_proj:
        q_ref = refs[1]
        q = jnp.dot(_rms(y, g2_ref[...]).astype(BF16), wq_ref[...], preferred_element_type=F32)
        for j in range(q.shape[1] // LANES):
            lanes = slice(j * LANES, (j + 1) * LANES)
            q_ref[:, lanes] = _rotate(q[:, lanes], rot_ref).astype(BF16)


def _ffn(h, g, wg, wu, wd, *, mixer=None, query_proj=None, final_g=None, tm=TOKEN_TILE):
    m = h.shape[0]
    row = pl.BlockSpec((tm, D_MODEL), lambda i: (i, 0))
    in_specs = [row, _resident((1, D_MODEL)), _resident((D_MODEL, D_FF)),
                _resident((D_MODEL, D_FF)), _resident((D_FF, D_MODEL))]
    args = [h, g.reshape(1, D_MODEL), wg, wu, wd]
    unpermute = False
    if mixer is not None:
        attn, w_o, unperm = mixer
        unpermute = unperm is not None
        if unpermute:
            tiles_per_seq = attn.shape[2] * A_CLASSES // tm
            in_specs.append(pl.BlockSpec((None, A_CLASSES, tm // A_CLASSES, ATTN_DIM),
                                         lambda i: (i // tiles_per_seq, 0, i % tiles_per_seq, 0)))
        else:
            in_specs.append(pl.BlockSpec((tm, ATTN_DIM), lambda i: (i, 0)))
        in_specs.append(_resident((ATTN_DIM, D_MODEL)))
        args += [attn, w_o]
        if unpermute:
            in_specs.append(_resident((PERM_SLAB, PERM_SLAB)))
            args.append(unperm)
    out_shape = [jax.ShapeDtypeStruct((m, D_MODEL), F32)]
    out_specs = [row]
    if query_proj is not None:
        g2, wq, rot = query_proj
        tiles_per_seq = rot.shape[1] // tm
        in_specs += [_resident((1, D_MODEL)), _resident(wq.shape),
                     pl.BlockSpec((3, tm, LANES), lambda i: (0, i % tiles_per_seq, 0))]
        args += [g2.reshape(1, D_MODEL), wq, rot]
        out_shape.append(jax.ShapeDtypeStruct((m, wq.shape[1]), BF16))
        out_specs.append(pl.BlockSpec((tm, wq.shape[1]), lambda i: (i, 0)))
    if final_g is not None:
        in_specs.append(_resident((1, D_MODEL)))
        args.append(final_g.reshape(1, D_MODEL))
    return pl.pallas_call(
        functools.partial(_ffn_body, mixer=mixer is not None, unpermute=unpermute,
                          query_proj=query_proj is not None, final_norm=final_g is not None),
        grid=(m // tm,),
        in_specs=in_specs,
        out_specs=out_specs,
        out_shape=out_shape,
        compiler_params=pltpu.CompilerParams(
            dimension_semantics=("parallel",), vmem_limit_bytes=VMEM_LIMIT),
        name="ffn",
    )(*args)


def _rotary_tables(positions, scale):
    half = ROT_DIM // 2
    n = len(positions)
    inv_freq = ROPE_THETA ** (-np.arange(half, dtype=np.float64) / half)
    ang = positions.astype(np.float64)[:, None] * inv_freq[None, :]
    cos, sin = np.cos(ang), np.sin(ang)
    zeros = np.zeros((n, HEAD_DIM - ROT_DIM))
    z8 = np.zeros((n, half))
    c = np.concatenate([cos, cos, np.ones_like(zeros)], axis=1)
    s_up = np.concatenate([z8, sin, zeros], axis=1)
    s_dn = np.concatenate([-sin, z8, zeros], axis=1)
    tab = np.stack([c, s_up, s_dn], axis=0) * scale
    return jnp.asarray(np.tile(tab, (1, 1, LANES // HEAD_DIM)), F32)


def _proj_body(*refs, groups_a, groups_b, shared_kv, class_major):
    if shared_kv:
        x_ref, g_ref, w_ref, rot_a_ref, rot_b_ref, wvt_ref, o_ref, km_ref, vt_ref = refs
    elif class_major:
        x_ref, g_ref, w_ref, rot_a_ref, rot_b_ref, perm_ref, o_ref = refs
    else:
        x_ref, g_ref, w_ref, rot_a_ref, rot_b_ref, o_ref = refs
    x = x_ref[...]
    hn = _rms(x, g_ref[...]).astype(BF16)
    if class_major:
        n_slabs = hn.shape[0] // PERM_SLAB
        hn = jnp.concatenate(
            [jnp.dot(perm_ref[...], hn[s * PERM_SLAB:(s + 1) * PERM_SLAB],
                     preferred_element_type=F32).astype(BF16) for s in range(n_slabs)], axis=0)
    y = jnp.dot(hn, w_ref[...], preferred_element_type=F32)
    tm, n = y.shape
    for j in range(n // LANES):
        t = y[:, j * LANES:(j + 1) * LANES]
        rot_ref = rot_a_ref if j < groups_a else (rot_b_ref if j < groups_a + groups_b else None)
        if rot_ref is not None:
            t = _rotate(t, rot_ref)
            if shared_kv:
                km_ref[:, 0, j * LANES:(j + 1) * LANES] = jnp.mean(
                    t.reshape(tm // MOBA_BLOCK, MOBA_BLOCK, LANES), axis=1)
        if class_major:
            per = PERM_SLAB // A_CLASSES
            for r in range(A_CLASSES):
                rows = [t[s * PERM_SLAB + r * per:s * PERM_SLAB + (r + 1) * per] for s in range(n_slabs)]
                o_ref[r, :, j * LANES:(j + 1) * LANES] = jnp.concatenate(rows, axis=0).astype(BF16)
        else:
            o_ref[:, j * LANES:(j + 1) * LANES] = t.astype(BF16)
    if shared_kv:
        vt = lax.dot_general(wvt_ref[...], hn, (((1,), (1,)), ((), ())), preferred_element_type=F32)
        for j in range(tm // MOBA_BLOCK):
            vt_ref[j] = vt[:, j * MOBA_BLOCK:(j + 1) * MOBA_BLOCK].astype(BF16)


def _proj(h, g, w, rot_a, rot_b, *, groups_a, groups_b, values_t_w=None, class_perm=None, tm=TOKEN_TILE):
    m = h.shape[0]
    n = w.shape[1]
    seq = rot_a.shape[1]
    tiles_per_seq = seq // tm
    shared_kv = values_t_w is not None
    class_major = class_perm is not None
    rot_spec = pl.BlockSpec((3, tm, LANES), lambda i: (0, i % tiles_per_seq, 0))
    in_specs = [pl.BlockSpec((tm, D_MODEL), lambda i: (i, 0)), _resident((1, D_MODEL)),
                _resident((D_MODEL, n)), rot_spec, rot_spec]
    args = [h, g.reshape(1, D_MODEL), w, rot_a, rot_b]
    out_shape = [jax.ShapeDtypeStruct((m, n), BF16)]
    out_specs = [pl.BlockSpec((tm, n), lambda i: (i, 0))]
    if shared_kv:
        nb = tm // MOBA_BLOCK
        in_specs.append(_resident((ATTN_DIM, D_MODEL)))
        args.append(values_t_w)
        out_shape += [jax.ShapeDtypeStruct((m // MOBA_BLOCK, 1, n), F32),
                      jax.ShapeDtypeStruct((m // MOBA_BLOCK, ATTN_DIM, MOBA_BLOCK), BF16)]
        out_specs += [pl.BlockSpec((nb, 1, n), lambda i: (i, 0, 0)),
                      pl.BlockSpec((nb, ATTN_DIM, MOBA_BLOCK), lambda i: (i, 0, 0))]
    if class_major:
        in_specs.append(_resident((PERM_SLAB, PERM_SLAB)))
        args.append(class_perm)
        out_shape = [jax.ShapeDtypeStruct((m // seq, A_CLASSES, seq // A_CLASSES, n), BF16)]
        out_specs = [pl.BlockSpec((None, A_CLASSES, tm // A_CLASSES, n),
                                  lambda i: (i // tiles_per_seq, 0, i % tiles_per_seq, 0))]
    return pl.pallas_call(
        functools.partial(_proj_body, groups_a=groups_a, groups_b=groups_b, shared_kv=shared_kv,
                          class_major=class_major),
        grid=(m // tm,),
        in_specs=in_specs,
        out_specs=out_specs,
        out_shape=out_shape,
        compiler_params=pltpu.CompilerParams(
            dimension_semantics=("parallel",), vmem_limit_bytes=VMEM_LIMIT),
        name="proj",
    )(*args)


def _first_head(rows):
    return lax.broadcasted_iota(jnp.int32, (rows, LANES), 1) < HEAD_DIM


def _band_scores(q, k):
    fq = _first_head(q.shape[0])
    zero = jnp.zeros_like(q)
    q_both = jnp.concatenate([jnp.where(fq, q, zero), jnp.where(fq, zero, q)], axis=0)
    return lax.dot_general(q_both, k, (((1,), (1,)), ((), ())), preferred_element_type=F32)


def _band_softmax(s, v, bias):
    nq, nk = bias.shape
    fq, fk = _first_head(nq), _first_head(nk)
    one = jnp.ones_like(v)
    outs, tops = [], []
    for head in range(2):
        sh = s[head * nq:(head + 1) * nq] + bias
        top = jnp.max(sh, axis=-1, keepdims=True)
        p = jnp.exp2(sh - top).astype(BF16)
        vh = jnp.where(fk, v, one) if head == 0 else jnp.where(fk, one, v)
        outs.append(jnp.dot(p, vh, preferred_element_type=F32))
        tops.append(top)
    acc = jnp.where(fq, outs[0], outs[1])
    den = pltpu.roll(jnp.where(fq, outs[1], outs[0]), HEAD_DIM, 1)
    return acc, jnp.where(fq, tops[0], tops[1]), den


def _merge_softmax(acc_a, top_a, den_a, acc_b, top_b, den_b):
    top = jnp.maximum(top_a, top_b)
    wa, wb = jnp.exp2(top_a - top), jnp.exp2(top_b - top)
    return wa * acc_a + wb * acc_b, top, wa * den_a + wb * den_b


def _attn_a_body(q_ref, k_ref, v_ref, b16_ref, b4_ref, b1_ref, o_ref, acc_ref, top_ref, den_ref):
    n_idx = q_ref.shape[1]

    def class_tiles(r0, carry):
        classes = [8 * r0 + u for u in range(8)]
        scores = [_band_scores(q_ref[r], k_ref[r]) for r in classes]
        for r, s in zip(classes, scores):
            acc_ref[r], top_ref[r], den_ref[r] = _band_softmax(s, v_ref[r], b16_ref[...])
        return carry

    lax.fori_loop(0, A_CLASSES // 8, class_tiles, 0)

    def tiles_d4(it_lo, carry):
        tiles = [(it, r4) for it in (it_lo, it_lo + n_idx // 64) for r4 in range(4)]

        def window(it, r4):
            i0 = pl.multiple_of(it * 32, 32)
            j0 = pl.multiple_of(jnp.maximum(it - 1, 0) * 32, 32)
            return i0, j0, [r4 + 4 * c for c in range(4)]

        scores = []
        for it, r4 in tiles:
            i0, j0, classes = window(it, r4)
            q = jnp.concatenate([q_ref[r, pl.ds(i0, 32), :] for r in classes], axis=0)
            k = jnp.concatenate([k_ref[r, pl.ds(j0, 64), :] for r in classes], axis=0)
            scores.append(_band_scores(q, k))
        for (it, r4), s in zip(tiles, scores):
            i0, j0, classes = window(it, r4)
            v = jnp.concatenate([v_ref[r, pl.ds(j0, 64), :] for r in classes], axis=0)
            acc, top, den = _band_softmax(s, v, b4_ref[jnp.minimum(it, 1)])
            for c, r in enumerate(classes):
                at = (r, pl.ds(i0, 32), slice(None))
                part = slice(c * 32, (c + 1) * 32)
                acc_ref[at], top_ref[at], den_ref[at] = _merge_softmax(
                    acc_ref[at], top_ref[at], den_ref[at], acc[part], top[part], den[part])
        return carry

    lax.fori_loop(0, n_idx // 64, tiles_d4, 0)

    def scores_d1(it):
        i0 = pl.multiple_of(it * 16, 16)
        j0 = pl.multiple_of(jnp.maximum(it - 1, 0) * 16, 16)
        q = jnp.concatenate([q_ref[r, pl.ds(i0, 16), :] for r in range(A_CLASSES)], axis=0)
        k = jnp.concatenate([k_ref[r, pl.ds(j0, 32), :] for r in range(A_CLASSES)], axis=0)
        return _band_scores(q, k)

    def finish_d1(it, first, s):
        i0 = pl.multiple_of(it * 16, 16)
        j0 = pl.multiple_of(jnp.maximum(it - 1, 0) * 16, 16)
        v = jnp.concatenate([v_ref[r, pl.ds(j0, 32), :] for r in range(A_CLASSES)], axis=0)
        acc, top, den = _band_softmax(s, v, b1_ref[first])
        for r in range(A_CLASSES):
            at = (r, pl.ds(i0, 16), slice(None))
            part = slice(r * 16, (r + 1) * 16)
            acc_m, _, den_m = _merge_softmax(
                acc_ref[at], top_ref[at], den_ref[at], acc[part], top[part], den[part])
            o_ref[at] = (acc_m / den_m).astype(BF16)

    quarter = n_idx // 64

    def tiles_d1(it, carry):
        tiles = [it + u * quarter for u in range(4)]
        scores = [scores_d1(t) for t in tiles]
        finish_d1(tiles[0], jnp.minimum(it, 1), scores[0])
        for t, s in zip(tiles[1:], scores[1:]):
            finish_d1(t, 1, s)
        return carry

    lax.fori_loop(0, quarter, tiles_d1, 0)


def _band_bias(n_q_classes, q_len, k_len, dilation_step, offset):
    cq = np.arange(n_q_classes * q_len) // q_len
    iq = np.arange(n_q_classes * q_len) % q_len
    ck = np.arange(n_q_classes * k_len) // k_len
    ik = np.arange(n_q_classes * k_len) % k_len
    dist = dilation_step * (offset + iq[:, None] - ik[None, :]) + (cq[:, None] - ck[None, :])
    return np.where((dist >= 0) & (dist <= BAND_BLOCK), 0.0, MASKED_SCORE).astype(np.float32)


def _attn_a(qkv, batch, seq):
    n_idx = seq // A_CLASSES
    b16 = jnp.asarray(_band_bias(1, n_idx, n_idx, 1, 0))
    b4 = jnp.asarray(np.stack([_band_bias(4, 32, 64, 4, off) for off in (0, 32)]))
    b1 = jnp.asarray(np.stack([_band_bias(A_CLASSES, 16, 32, A_CLASSES, off) for off in (0, 16)]))

    def spec(which):
        return pl.BlockSpec((None, A_CLASSES, n_idx, LANES),
                            lambda b, hp: (b, 0, 0, which * N_HEAD_PAIRS + hp))

    return pl.pallas_call(
        _attn_a_body,
        grid=(batch, N_HEAD_PAIRS),
        in_specs=[spec(0), spec(1), spec(2),
                  _resident(b16.shape), _resident(b4.shape), _resident(b1.shape)],
        out_specs=pl.BlockSpec((None, A_CLASSES, n_idx, LANES), lambda b, hp: (b, 0, 0, hp)),
        out_shape=jax.ShapeDtypeStruct((batch, A_CLASSES, n_idx, ATTN_DIM), BF16),
        scratch_shapes=[pltpu.VMEM((A_CLASSES, n_idx, LANES), F32)] * 3,
        compiler_params=pltpu.CompilerParams(
            dimension_semantics=("parallel", "parallel"), vmem_limit_bytes=VMEM_LIMIT),
        name="attn_a",
    )(qkv, qkv, qkv, b16, b4, b1)


def _class_perm(tm):
    per = tm // A_CLASSES
    dst = np.arange(tm)
    src = A_CLASSES * (dst % per) + dst // per
    return (src[:, None] == np.arange(tm)[None, :]).astype(np.float32)


MOBA_GROUP = 4
ONES_ROWS = 16
MOBA_PAIRS_PER_STEP = 4


def _moba_body(q_ref, k_ref, vt_ref, tag_ref, km_ref, o_ref, s_ref):
    own = pl.program_id(2)
    rows = q_ref.shape[0]
    n_blocks = km_ref.shape[0]
    cols = 2 * rows
    group_keys = MOBA_GROUP * MOBA_BLOCK
    contract_last = (((1,), (1,)), ((), ()))
    first_head = _first_head(rows)
    blk = lax.broadcasted_iota(jnp.int32, (n_blocks, cols), 0)
    key_pos = lax.broadcasted_iota(jnp.int32, (MOBA_BLOCK, cols), 0)
    query_pos = lax.broadcasted_iota(jnp.int32, (MOBA_BLOCK, cols), 1) & (rows - 1)
    own0 = pl.multiple_of(own * MOBA_BLOCK, MOBA_BLOCK)

    def values_t(n0, n, pair, head):
        lo = pair * LANES + head * HEAD_DIM
        vt = jnp.concatenate([vt_ref[n0 + u, lo:lo + HEAD_DIM, :] for u in range(n)], axis=1)
        return jnp.concatenate([vt, jnp.ones((ONES_ROWS, n * MOBA_BLOCK), BF16)], axis=0)

    def weighted_values(n0, n, pair, p):
        return [jnp.dot(values_t(n0, n, pair, head), p[:, head * rows:(head + 1) * rows],
                        preferred_element_type=F32) for head in range(2)]

    pairs = range(MOBA_PAIRS_PER_STEP)
    q_both, gates, own_scores = [], [], []
    for pair in pairs:
        lanes = slice(pair * LANES, (pair + 1) * LANES)
        q = q_ref[:, lanes]
        zero = jnp.zeros_like(q)
        q_both.append(jnp.concatenate([jnp.where(first_head, q, zero), jnp.where(first_head, zero, q)], axis=0))
        gates.append(lax.dot_general(km_ref[:, lanes].astype(BF16), q_both[pair], contract_last,
                                     preferred_element_type=F32))
        own_scores.append(lax.dot_general(k_ref[pl.ds(own0, MOBA_BLOCK), lanes], q_both[pair], contract_last,
                                          preferred_element_type=F32))
        s_ref[pair] = lax.dot_general(k_ref[pl.ds(0, group_keys), lanes], q_both[pair], contract_last,
                                      preferred_element_type=F32)

    q_augs, own_top, own_p, first_max = [], [], [], []
    for pair in pairs:
        gate = jnp.where(blk < own, gates[pair], -jnp.inf)
        chosen = jnp.zeros(gate.shape, jnp.bool_)
        for _ in range(MOBA_TOPK):
            best = jnp.max(gate, axis=0, keepdims=True)
            at = jnp.min(jnp.where(gate == best, blk, n_blocks), axis=0, keepdims=True)
            hit = blk == at
            chosen = chosen | (hit & (best > -jnp.inf))
            gate = jnp.where(hit, -jnp.inf, gate)
        bias_t = jnp.concatenate([jnp.where(chosen, 0.0, MASKED_SCORE),
                                  jnp.full((LANES - n_blocks, cols), MASKED_SCORE, F32)], axis=0)
        q_augs.append(jnp.concatenate([q_both[pair], bias_t.T.astype(BF16)], axis=1))
        first_bias = jnp.concatenate([jnp.broadcast_to(bias_t[u:u + 1], (MOBA_BLOCK, cols))
                                      for u in range(MOBA_GROUP)], axis=0)
        s = s_ref[pair] + first_bias
        s_ref[pair] = s
        first_max.append(jnp.max(s, axis=0, keepdims=True))

        s = jnp.where(key_pos <= query_pos, own_scores[pair], MASKED_SCORE)
        own_top.append(jnp.max(s, axis=0, keepdims=True))
        own_p.append(jnp.exp2(s - own_top[pair]).astype(BF16))

    init = []
    for pair in pairs:
        acc0 = weighted_values(own, 1, pair, own_p[pair])
        init.append((own_top[pair], acc0, first_max[pair]))

    def scores(n0, pair, s_ref):
        k0 = pl.multiple_of(n0 * MOBA_BLOCK, group_keys)
        k_aug = jnp.concatenate([k_ref[pl.ds(k0, group_keys), pair * LANES:(pair + 1) * LANES],
                                 tag_ref[pl.ds(k0, group_keys), :]], axis=1)
        s = lax.dot_general(k_aug, q_augs[pair], contract_last, preferred_element_type=F32)
        s_ref[pair] = s
        return jnp.max(s, axis=0, keepdims=True)

    def absorb_group(j, carry, score_next, n_used=MOBA_GROUP):
        n0 = MOBA_GROUP * j
        out = []
        for pair, (m, acc, s_max) in enumerate(carry):
            m_new = jnp.maximum(m, s_max)
            p = jnp.exp2(s_ref[pair, :n_used * MOBA_BLOCK, :] - m_new).astype(BF16)
            alpha = jnp.exp2(m - m_new)
            if score_next:
                s_max = scores(n0 + MOBA_GROUP, pair, s_ref)
            pv = weighted_values(n0, n_used, pair, p)
            acc = [alpha[:, head * rows:(head + 1) * rows] * acc[head] + pv[head] for head in range(2)]
            out.append((m_new, acc, s_max))
        return out

    last_group = jnp.maximum((own + MOBA_GROUP - 1) // MOBA_GROUP - 1, 0)
    state = lax.fori_loop(0, last_group, functools.partial(absorb_group, score_next=True), init)
    final = lax.cond(
        own - MOBA_GROUP * last_group <= MOBA_GROUP // 2,
        functools.partial(absorb_group, last_group, score_next=False, n_used=MOBA_GROUP // 2),
        functools.partial(absorb_group, last_group, score_next=False),
        state)
    for pair, (_, acc, _) in enumerate(final):
        out_t = [acc[head][:HEAD_DIM] * (1.0 / acc[head][HEAD_DIM:HEAD_DIM + 1]) for head in range(2)]
        o_ref[:, pair * LANES:(pair + 1) * LANES] = jnp.concatenate(out_t, axis=0).T.astype(BF16)


def _moba(q, k, vt, key_means, batch, seq):
    n_blocks = seq // MOBA_BLOCK
    q3 = q.reshape(batch, seq, ATTN_DIM)
    k3 = k.reshape(batch, seq, ATTN_DIM)
    km3 = key_means.reshape(batch, n_blocks, ATTN_DIM)
    tag = jnp.asarray(np.arange(seq)[:, None] // MOBA_BLOCK == np.arange(LANES)[None, :], BF16)
    width = MOBA_PAIRS_PER_STEP * LANES
    o = pl.pallas_call(
        _moba_body,
        grid=(batch, ATTN_DIM // width, n_blocks),
        in_specs=[pl.BlockSpec((None, MOBA_BLOCK, width), lambda b, hp, i: (b, i, hp)),
                  pl.BlockSpec((None, seq, width), lambda b, hp, i: (b, 0, hp)),
                  pl.BlockSpec((n_blocks, width, MOBA_BLOCK), lambda b, hp, i: (b, hp, 0)),
                  _resident((seq, LANES)),
                  pl.BlockSpec((None, n_blocks, width), lambda b, hp, i: (b, 0, hp))],
        out_specs=pl.BlockSpec((None, MOBA_BLOCK, width), lambda b, hp, i: (b, i, hp)),
        out_shape=jax.ShapeDtypeStruct((batch, seq, ATTN_DIM), BF16),
        scratch_shapes=[pltpu.VMEM((MOBA_PAIRS_PER_STEP, MOBA_GROUP * MOBA_BLOCK, 2 * MOBA_BLOCK), F32)],
        compiler_params=pltpu.CompilerParams(
            dimension_semantics=("parallel", "parallel", "arbitrary"), vmem_limit_bytes=VMEM_LIMIT),
        name="moba",
    )(q3, k3, vt, tag, km3)
    return o.reshape(batch * seq, ATTN_DIM)


def kernel(x, ffn1_norm, ffn1_w_gate, ffn1_w_up, ffn1_w_down, mix_norm, ffn2_norm, ffn2_w_gate,
           ffn2_w_up, ffn2_w_down, a_w_qkv, a_w_o, kv_norm, kv_w, b_w_q, b_w_o, final_norm):
    batch, seq, _ = x.shape
    bf = lambda w: w.astype(BF16)
    positions = np.arange(seq)
    per = PERM_SLAB // A_CLASSES
    class_major = positions.reshape(seq // PERM_SLAB, per, A_CLASSES).transpose(0, 2, 1).reshape(seq)
    perm = _class_perm(PERM_SLAB)
    h = x.reshape(batch * seq, D_MODEL)

    h = _ffn(h, ffn1_norm[0], bf(ffn1_w_gate[0]), bf(ffn1_w_up[0]), bf(ffn1_w_down[0]))[0]
    qkv = _proj(h, mix_norm[0], bf(a_w_qkv[0]), _rotary_tables(class_major, QUERY_SCALE),
                _rotary_tables(class_major, 1.0), groups_a=N_HEAD_PAIRS, groups_b=N_HEAD_PAIRS,
                class_perm=jnp.asarray(perm, BF16))[0]
    h = _ffn(h, ffn2_norm[0], bf(ffn2_w_gate[0]), bf(ffn2_w_up[0]), bf(ffn2_w_down[0]),
             mixer=(_attn_a(qkv, batch, seq), bf(a_w_o[0]), jnp.asarray(perm.T, BF16)))[0]

    rot_k = _rotary_tables(positions, 1.0)
    k, key_means, vt = _proj(h, kv_norm, bf(kv_w[:, :ATTN_DIM]), rot_k, rot_k, groups_a=0,
                             groups_b=N_HEAD_PAIRS, values_t_w=bf(kv_w[:, ATTN_DIM:].T))

    h, q = _ffn(h, ffn1_norm[1], bf(ffn1_w_gate[1]), bf(ffn1_w_up[1]), bf(ffn1_w_down[1]),
                query_proj=(mix_norm[1], bf(b_w_q[0]), _rotary_tables(positions, QUERY_SCALE)))
    h = _ffn(h, ffn2_norm[1], bf(ffn2_w_gate[1]), bf(ffn2_w_up[1]), bf(ffn2_w_down[1]),
             mixer=(_moba(q, k, vt, key_means, batch, seq), bf(b_w_o[0]), None), final_g=final_norm)[0]
    return h.reshape(batch, seq, D_MODEL)
```
